```python
import math
import jax, jax.numpy as jnp
from jax import lax
import numpy as np

D_MODEL = 1024
BATCH = 4
SEQ = 8192
DEPTH = 4
DEC_BATCH = 32
DEC_SEQ = 2048
PAST_LEN = 128

HEAD_DIM = 64
HEADS_PER_GROUP = 4
DILATED_GROUPS = ((128, 1), (512, 4), (2048, 16))
N_GROUPS_A = 3
N_HEADS_A = N_GROUPS_A * HEADS_PER_GROUP
ATTN_WIDTH = N_HEADS_A * HEAD_DIM
ATTN_OUT = HEADS_PER_GROUP * HEAD_DIM
NUM_BUCKETS = 32
MAX_DISTANCE = 1024
SG_CHUNK = 128
SG_GROUPS = 8
SG_WIDTH = 1024
SG_GROUP_CH = SG_WIDTH // SG_GROUPS
D_FF = 2816
CONV_WIDTH = 3
IN_WIDTH = 3 * ATTN_WIDTH + 2 * SG_WIDTH + 2 * D_MODEL
EPS = 1e-6

kernel_name = "hybrid_dilated_attn_spatial_gating_encoder"


def rmsnorm(x, g):
    xf = x.astype(jnp.float32)
    y = xf * lax.rsqrt(jnp.mean(xf * xf, axis=-1, keepdims=True) + EPS)
    return (y * g.astype(jnp.float32)).astype(x.dtype)


def t5_bucket(rel):
    nb = NUM_BUCKETS // 2
    max_exact = nb // 2
    ret = np.where(rel > 0, nb, 0)
    n = np.abs(rel)
    nf = np.maximum(n, 1).astype(np.float32)
    large = max_exact + (np.log(nf / max_exact) / math.log(MAX_DISTANCE / max_exact) * (nb - max_exact)).astype(np.int32)
    large = np.minimum(large, nb - 1)
    return (ret + np.where(n < max_exact, n, large)).astype(np.int32)


def dilated_window_attn(q, k, v, bias_tab, dil, half):
    B, S, H, Dh = q.shape
    L = S // dil
    qb_size = half
    nblk = -(-L // qb_size)
    Lp = nblk * qb_size

    def to_sub(t):
        t = t.reshape(B, L, dil, H, Dh).transpose(0, 2, 1, 3, 4)
        return jnp.pad(t, ((0, 0), (0, 0), (0, Lp - L), (0, 0), (0, 0)))

    def neighbourhood(t):
        t = jnp.pad(t, ((0, 0), (0, 0), (qb_size, qb_size), (0, 0), (0, 0)))
        t = t.reshape(B, dil, nblk + 2, qb_size, H, Dh)
        return jnp.concatenate([t[:, :, :-2], t[:, :, 1:-1], t[:, :, 2:]], axis=3)

    qs = to_sub(q).reshape(B, dil, nblk, qb_size, H, Dh)
    kw = neighbourhood(to_sub(k))
    vw = neighbourhood(to_sub(v))

    p_idx = np.arange(qb_size)[:, None]
    c_idx = np.arange(3 * qb_size)[None, :]
    rel_sub = c_idx - qb_size - p_idx
    band = np.abs(rel_sub) <= half
    key_pos = (np.arange(nblk)[:, None] - 1) * qb_size + np.arange(3 * qb_size)[None, :]
    kvalid = (key_pos >= 0) & (key_pos < L)
    mask = jnp.asarray(band[None] & kvalid[:, None, :])
    bucket = jnp.asarray(t5_bucket(rel_sub * dil))
    bias = jnp.transpose(bias_tab.astype(jnp.float32)[bucket], (2, 0, 1))

    scale = Dh ** -0.5
    s = jnp.einsum('bgnqhd,bgnkhd->bgnhqk', qs.astype(jnp.float32), kw.astype(jnp.float32)) * scale + bias
    s = jnp.where(mask[None, None, :, None], s, -1e30)
    m = jnp.max(s, axis=-1, keepdims=True)
    p = jnp.exp(s - m)
    denom = jnp.sum(p, axis=-1, keepdims=True)
    o = jnp.einsum('bgnhqk,bgnkhd->bgnqhd', p / denom, vw.astype(jnp.float32))
    lse = jnp.transpose((m + jnp.log(denom))[..., 0], (0, 1, 2, 4, 3))

    def from_sub(t):
        rest = t.shape[4:]
        t = t.reshape((B, dil, Lp) + rest)[:, :, :L]
        t = jnp.moveaxis(t, 1, 2)
        return t.reshape((B, S) + rest)

    return from_sub(o), from_sub(lse)


def spatial_gating(u, v, v_gain, w_s, b_s):
    B, S, C = v.shape
    vn = rmsnorm(v, v_gain).reshape(B, S // SG_CHUNK, SG_CHUNK, SG_GROUPS, SG_GROUP_CH)
    z = jnp.einsum('gpq,bnqgc->bnpgc', w_s, vn) + b_s.T[:, :, None]
    return u * z.reshape(B, S, C)


def token_mixer(h, rel_bias, w_in, v_gain, w_s, b_s, w_proj_a, w_proj_b, w_out):
    B, S, _ = h.shape
    proj = jnp.einsum('bsd,de->bse', h, w_in)
    splits = [ATTN_WIDTH, 2 * ATTN_WIDTH, 3 * ATTN_WIDTH, 3 * ATTN_WIDTH + 2 * SG_WIDTH]
    q, k, v, uv, gates = jnp.split(proj, splits, axis=-1)
    shp = (B, S, N_GROUPS_A, HEADS_PER_GROUP, HEAD_DIM)
    q, k, v = q.reshape(shp), k.reshape(shp), v.reshape(shp)

    outs, lses = [], []
    for g, (win, dil) in enumerate(DILATED_GROUPS):
        o_g, l_g = dilated_window_attn(q[:, :, g], k[:, :, g], v[:, :, g],
                                       rel_bias[:, g * HEADS_PER_GROUP:(g + 1) * HEADS_PER_GROUP],
                                       dil, win // (2 * dil))
        outs.append(o_g)
        lses.append(l_g)
    wts = jax.nn.softmax(jnp.stack(lses, axis=0), axis=0)
    o_a = jnp.sum(wts[..., None] * jnp.stack(outs, axis=0), axis=0)
    y_a = jnp.einsum('bse,ed->bsd', o_a.reshape(B, S, ATTN_OUT).astype(h.dtype), w_proj_a)

    uv = jax.nn.gelu(uv)
    u, vv = jnp.split(uv, 2, axis=-1)
    y_b = jnp.einsum('bse,ed->bsd', spatial_gating(u, vv, v_gain, w_s, b_s), w_proj_b)

    g_a, g_b = jnp.split(gates, 2, axis=-1)
    merged = jax.nn.sigmoid(g_a) * y_a + jax.nn.sigmoid(g_b) * y_b
    return jnp.einsum('bsd,de->bse', merged, w_out)


def conv_ffn(h, w_up, conv_w, conv_b, w_down):
    a = jnp.einsum('bsd,df->bsf', h, w_up)
    ap = jnp.pad(a, ((0, 0), (1, 1), (0, 0)))
    a = ap[:, :-2] * conv_w[0] + ap[:, 1:-1] * conv_w[1] + ap[:, 2:] * conv_w[2] + conv_b
    gate, val = jnp.split(a, 2, axis=-1)
    return jnp.einsum('bsf,fd->bsd', jax.nn.gelu(gate) * val, w_down)


def trunk(x, rel_bias, norm_mix, w_in, v_gain, w_s, b_s, w_proj_a, w_proj_b, w_out,
          norm_ffn, w_up, conv_w, conv_b, w_down, norm_final):
    for l in range(DEPTH):
        x = x + token_mixer(rmsnorm(x, norm_mix[l]), rel_bias, w_in[l], v_gain[l], w_s[l], b_s[l],
                            w_proj_a[l], w_proj_b[l], w_out[l])
        x = x + conv_ffn(rmsnorm(x, norm_ffn[l]), w_up[l], conv_w[l], conv_b[l], w_down[l])
    return rmsnorm(x, norm_final)


def setup_inputs(seed: int = 0) -> dict:
    key = jax.random.key(seed)
    ks = jax.random.split(key, 20)
    f32 = jnp.float32
    res_scale = (2.0 * DEPTH) ** -0.5
    nrm = lambda k, shp, s: jax.random.normal(k, shp, f32) * s
    return {
        "x_prompt": nrm(ks[0], (BATCH, SEQ, D_MODEL), 1.0),
        "x_sample": nrm(ks[1], (DEC_BATCH, DEC_SEQ, D_MODEL), 1.0),
        "rel_bias": nrm(ks[2], (NUM_BUCKETS, N_HEADS_A), 0.5),
        "norm_mix": 1.0 + nrm(ks[3], (DEPTH, D_MODEL), 0.02),
        "w_in": nrm(ks[4], (DEPTH, D_MODEL, IN_WIDTH), D_MODEL ** -0.5),
        "v_gain": 1.0 + nrm(ks[5], (DEPTH, SG_WIDTH), 0.02),
        "w_s": nrm(ks[6], (DEPTH, SG_GROUPS, SG_CHUNK, SG_CHUNK), SG_CHUNK ** -0.5),
        "b_s": 1.0 + nrm(ks[7], (DEPTH, SG_GROUPS, SG_CHUNK), 0.02),
        "w_proj_a": nrm(ks[8], (DEPTH, ATTN_OUT, D_MODEL), ATTN_OUT ** -0.5),
        "w_proj_b": nrm(ks[9], (DEPTH, SG_WIDTH, D_MODEL), SG_WIDTH ** -0.5),
        "w_out": nrm(ks[10], (DEPTH, D_MODEL, D_MODEL), D_MODEL ** -0.5 * res_scale),
        "norm_ffn": 1.0 + nrm(ks[11], (DEPTH, D_MODEL), 0.02),
        "w_up": nrm(ks[12], (DEPTH, D_MODEL, 2 * D_FF), D_MODEL ** -0.5),
        "conv_w": nrm(ks[13], (DEPTH, CONV_WIDTH, 2 * D_FF), CONV_WIDTH ** -0.5),
        "conv_b": nrm(ks[14], (DEPTH, 2 * D_FF), 0.02),
        "w_down": nrm(ks[15], (DEPTH, D_FF, D_MODEL), D_FF ** -0.5 * res_scale),
        "norm_final": 1.0 + nrm(ks[16], (D_MODEL,), 0.02),
    }


def reference(x_prompt, x_sample, rel_bias, norm_mix, w_in, v_gain, w_s, b_s, w_proj_a, w_proj_b,
              w_out, norm_ffn, w_up, conv_w, conv_b, w_down, norm_final):
    y_prompt = trunk(x_prompt, rel_bias, norm_mix, w_in, v_gain, w_s, b_s, w_proj_a, w_proj_b, w_out,
                     norm_ffn, w_up, conv_w, conv_b, w_down, norm_final)
    y_sample = trunk(x_sample, rel_bias, norm_mix, w_in, v_gain, w_s, b_s, w_proj_a, w_proj_b, w_out,
                     norm_ffn, w_up, conv_w, conv_b, w_down, norm_final)
    return (y_prompt, y_sample)
```

```python
import functools
import math

import jax
import jax.numpy as jnp
import numpy as np
from jax import lax
from jax.experimental import pallas as pl
from jax.experimental.pallas import tpu as pltpu

F32 = jnp.float32
BF16 = jnp.bfloat16

HEAD_DIM = 64
HEADS_PER_GROUP = 4
GROUP_WIDTH = HEADS_PER_GROUP * HEAD_DIM
DILATED_GROUPS = ((128, 1), (512, 4), (2048, 16))
N_GROUPS = len(DILATED_GROUPS)
HALF_WINDOW = 64
NUM_BUCKETS = 32
MAX_DISTANCE = 1024
SG_CHUNK = 128
SG_GROUPS = 8
CONV_WIDTH = 3
EPS = 1e-6
MASK_VALUE = -1e30

LANES = 128
SUBLANES = 8
BF16_ROWS = 16
VMEM_LIMIT_BYTES = 56 * 1024 * 1024

ROW_TILE = 512
ATTN_TILE = 2048
Q_BLOCK = 128
KEY_BLOCK = Q_BLOCK + 2 * HALF_WINDOW
FF_CHUNK = 256
HALO = SUBLANES
MERGE_ROWS = 256


def _rmsnorm(x, gain):
    y = x * lax.rsqrt(jnp.mean(x * x, axis=-1, keepdims=True) + EPS)
    return y * gain


def _gelu_tanh(x):
    c = math.sqrt(2.0 / math.pi)
    return 0.5 * x * (1.0 + jnp.tanh(c * (x + 0.044715 * (x * x * x))))


def _sigmoid(x):
    return 1.0 / (1.0 + jnp.exp(-x))


def _const_spec(shape):
    zeros = (0,) * len(shape)
    return pl.BlockSpec(shape, lambda *_: zeros, pipeline_mode=pl.Buffered(1))


def _params(n_grid_axes):
    return pltpu.CompilerParams(
        dimension_semantics=("arbitrary",) * n_grid_axes,
        vmem_limit_bytes=VMEM_LIMIT_BYTES,
    )


def _mixer_in_body(x_ref, nm_ref, wq0_ref, wq1_ref, wq2_ref, wuv_ref, wg_ref, vgain_ref, ws_ref,
                   zb_ref, wpb_ref, qkv0_ref, qkv1_ref, qkv2_ref, sa_ref, mb_ref, perm_ref):
    tm = x_ref.shape[1]
    width = vgain_ref.shape[1]
    h = _rmsnorm(x_ref[0], nm_ref[...]).astype(BF16)

    col = lax.broadcasted_iota(jnp.int32, (1, 3 * GROUP_WIDTH), 1)
    qscale = jnp.where(col < GROUP_WIDTH, HEAD_DIM ** -0.5, 1.0).astype(F32)
    for (_, dil), w_ref, out_ref in zip(DILATED_GROUPS, (wq0_ref, wq1_ref, wq2_ref),
                                        (qkv0_ref, qkv1_ref, qkv2_ref)):
        r = jnp.dot(h, w_ref[...], preferred_element_type=F32) * qscale
        if dil == 1:
            out_ref[0, 0] = r.astype(BF16)
        else:
            for c in range(perm_ref.shape[0]):
                perm_ref[c] = r[:, c * LANES:(c + 1) * LANES]
            for res in range(dil):
                for c in range(perm_ref.shape[0]):
                    out_ref[0, res, :, c * LANES:(c + 1) * LANES] = (
                        perm_ref[c, pl.ds(res, tm // dil, stride=dil), :].astype(BF16))

    uv = _gelu_tanh(jnp.dot(h, wuv_ref[...], preferred_element_type=F32))
    u = uv[:, :width]
    vn = _rmsnorm(uv[:, width:], vgain_ref[...]).astype(BF16)
    gch = width // SG_GROUPS
    uz = []
    for c in range(tm // SG_CHUNK):
        rows = slice(c * SG_CHUNK, (c + 1) * SG_CHUNK)
        z = jnp.concatenate(
            [jnp.dot(ws_ref[g], vn[rows, g * gch:(g + 1) * gch], preferred_element_type=F32)
             for g in range(SG_GROUPS)], axis=1)
        uz.append((u[rows] * (z + zb_ref[...])).astype(BF16))
    y_b = jnp.dot(jnp.concatenate(uz, axis=0), wpb_ref[...], preferred_element_type=F32)

    gates = jnp.dot(h, wg_ref[...], preferred_element_type=F32)
    d_model = gates.shape[1] // 2
    sa_ref[0] = _sigmoid(gates[:, :d_model]).astype(BF16)
    mb_ref[0] = (_sigmoid(gates[:, d_model:]) * y_b).astype(BF16)


def _mixer_in(x, nm, wq, wuv, wg, vgain, ws, zb, wpb):
    b, s, d_model = x.shape
    tm = min(ROW_TILE, s)
    assert s % tm == 0 and tm % SG_CHUNK == 0
    max_dil = max(d for _, d in DILATED_GROUPS)
    assert tm % (max_dil * BF16_ROWS) == 0
    qkv_w = 3 * GROUP_WIDTH
    out_shape = [jax.ShapeDtypeStruct((b, dil, s // dil, qkv_w), BF16) for _, dil in DILATED_GROUPS]
    out_shape += [jax.ShapeDtypeStruct((b, s, d_model), BF16)] * 2
    out_specs = [pl.BlockSpec((1, dil, tm // dil, qkv_w), lambda bi, ti: (bi, 0, ti, 0))
                 for _, dil in DILATED_GROUPS]
    out_specs += [pl.BlockSpec((1, tm, d_model), lambda bi, ti: (bi, ti, 0))] * 2
    in_specs = [pl.BlockSpec((1, tm, d_model), lambda bi, ti: (bi, ti, 0)), _const_spec(nm.shape)]
    in_specs += [_const_spec(w.shape) for w in wq]
    in_specs += [_const_spec(a.shape) for a in (wuv, wg, vgain, ws, zb, wpb)]
    return pl.pallas_call(
        _mixer_in_body,
        grid=(b, s // tm),
        in_specs=in_specs,
        out_specs=out_specs,
        out_shape=out_shape,
        scratch_shapes=[pltpu.VMEM((qkv_w // LANES, tm, LANES), F32)],
        compiler_params=_params(2),
        name="mixer_in",
    )(x, nm, *wq, wuv, wg, vgain, ws, zb, wpb)


def _attention_body(*refs, seq_len):
    n_in = 8 * N_GROUPS
    in_refs, out_ref, scratch = refs[:n_in], refs[n_in], refs[n_in + 1:]
    ti = pl.program_id(1)
    n_tiles = pl.num_programs(1)
    tq = out_ref.shape[1]
    lane = lax.broadcasted_iota(jnp.int32, (1, LANES), 1)
    low_half = lane < HEAD_DIM
    key_col = lax.broadcasted_iota(jnp.int32, (1, KEY_BLOCK), 1)

    for g, (_, dil) in enumerate(DILATED_GROUPS):
        (q_ref, k_ref, kp_ref, kn_ref, v_ref, vp_ref, vn_ref, bias_ref) = in_refs[8 * g:8 * g + 8]
        kx_ref, vx_ref, o_ref, lse_ref = scratch[4 * g:4 * g + 4]
        tl = tq // dil
        n_blocks = tl // Q_BLOCK
        sub_len = seq_len // dil

        kx_ref[:, :HALF_WINDOW, :] = kp_ref[0]
        kx_ref[:, HALF_WINDOW:HALF_WINDOW + tl, :] = k_ref[0]
        kx_ref[:, HALF_WINDOW + tl:, :] = kn_ref[0]
        vx_ref[:, :HALF_WINDOW, :] = vp_ref[0]
        vx_ref[:, HALF_WINDOW:HALF_WINDOW + tl, :] = v_ref[0]
        vx_ref[:, HALF_WINDOW + tl:, :] = vn_ref[0]

        def block(it, carry, *, dil=dil, n_blocks=n_blocks, tl=tl, sub_len=sub_len, q_ref=q_ref,
                  kx_ref=kx_ref, vx_ref=vx_ref, bias_ref=bias_ref, o_ref=o_ref, lse_ref=lse_ref):
            res = it // n_blocks
            jb = it % n_blocks
            row0 = pl.multiple_of(jb * Q_BLOCK, Q_BLOCK)
            q_blk = q_ref[0, res, pl.ds(row0, Q_BLOCK), :]
            k_win = kx_ref[res, pl.ds(row0, KEY_BLOCK), :]
            v_win = vx_ref[res, pl.ds(row0, KEY_BLOCK), :]
            key_pos = ti * tl + row0 - HALF_WINDOW + key_col
            key_ok = (key_pos >= 0) & (key_pos < sub_len)
            outs, lses = [], []
            for hp in range(HEADS_PER_GROUP // 2):
                cols = slice(hp * LANES, (hp + 1) * LANES)
                q_pair = q_blk[:, cols]
                zero = jnp.zeros_like(q_pair)
                q_stack = jnp.concatenate(
                    [jnp.where(low_half, q_pair, zero), jnp.where(low_half, zero, q_pair)], axis=0)
                s = lax.dot_general(q_stack, k_win[:, cols], (((1,), (1,)), ((), ())),
                                    preferred_element_type=F32)
                s = jnp.where(key_ok, s + bias_ref[hp], MASK_VALUE)
                m = jnp.max(s, axis=-1, keepdims=True)
                p = jnp.exp(s - m)
                denom = jnp.sum(p, axis=-1, keepdims=True)
                pv = jnp.dot(p.astype(BF16), v_win[:, cols], preferred_element_type=F32)
                inv = 1.0 / denom
                lse = m + jnp.log(denom)
                outs.append(jnp.where(low_half, pv[:Q_BLOCK] * inv[:Q_BLOCK],
                                      pv[Q_BLOCK:] * inv[Q_BLOCK:]))
                lses.append(jnp.where(low_half, lse[:Q_BLOCK], lse[Q_BLOCK:]))
            o_blk = jnp.concatenate(outs, axis=1)
            lse_blk = jnp.concatenate(lses, axis=1)
            if dil == 1:
                o_ref[pl.ds(row0, Q_BLOCK), :] = o_blk
                lse_ref[pl.ds(row0, Q_BLOCK), :] = lse_blk
            else:
                start = row0 * dil + res
                for c in range(GROUP_WIDTH // LANES):
                    o_ref[c, pl.ds(start, Q_BLOCK, stride=dil), :] = o_blk[:, c * LANES:(c + 1) * LANES]
                    lse_ref[c, pl.ds(start, Q_BLOCK, stride=dil), :] = lse_blk[:, c * LANES:(c + 1) * LANES]
            return carry

        lax.fori_loop(0, dil * n_blocks, block, 0)

    o0_ref, l0_ref = scratch[2], scratch[3]
    o1_ref, l1_ref = scratch[6], scratch[7]
    o2_ref, l2_ref = scratch[10], scratch[11]

    def merge(it, carry):
        row0 = pl.multiple_of(it * MERGE_ROWS, MERGE_ROWS)
        rows = pl.ds(row0, MERGE_ROWS)
        for c in range(GROUP_WIDTH // LANES):
            cols = slice(c * LANES, (c + 1) * LANES)
            l0, l1, l2 = l0_ref[rows, cols], l1_ref[c, rows, :], l2_ref[c, rows, :]
            mx = jnp.maximum(jnp.maximum(l0, l1), l2)
            w0, w1, w2 = jnp.exp(l0 - mx), jnp.exp(l1 - mx), jnp.exp(l2 - mx)
            num = w0 * o0_ref[rows, cols] + w1 * o1_ref[c, rows, :] + w2 * o2_ref[c, rows, :]
            out_ref[0, rows, cols] = (num / (w0 + w1 + w2)).astype(BF16)
        return carry

    lax.fori_loop(0, tq // MERGE_ROWS, merge, 0)


def _attention(qkvs, biases, seq_len):
    b = qkvs[0].shape[0]
    tq = min(ATTN_TILE, seq_len)
    assert seq_len % tq == 0
    in_specs, args, scratch = [], [], []
    for (_, dil), qkv, bias in zip(DILATED_GROUPS, qkvs, biases):
        tl = tq // dil
        assert tl % Q_BLOCK == 0
        halo_per_tile = tl // HALF_WINDOW
        n_halo = seq_len // dil // HALF_WINDOW

        def main(col):
            return pl.BlockSpec((1, dil, tl, GROUP_WIDTH), lambda bi, ti, col=col: (bi, 0, ti, col))

        def before(col, hpt=halo_per_tile):
            return pl.BlockSpec((1, dil, HALF_WINDOW, GROUP_WIDTH),
                                lambda bi, ti, col=col: (bi, 0, jnp.maximum(ti * hpt - 1, 0), col))

        def after(col, hpt=halo_per_tile, n_halo=n_halo):
            return pl.BlockSpec((1, dil, HALF_WINDOW, GROUP_WIDTH),
                                lambda bi, ti, col=col: (bi, 0, jnp.minimum((ti + 1) * hpt, n_halo - 1), col))

        in_specs += [main(0), main(1), before(1), after(1), main(2), before(2), after(2),
                     _const_spec(bias.shape)]
        args += [qkv] * 7 + [bias]
        ext = (dil, tl + 2 * HALF_WINDOW, GROUP_WIDTH)
        if dil == 1:
            out_scr = (tq, GROUP_WIDTH)
        else:
            out_scr = (GROUP_WIDTH // LANES, tq, LANES)
        scratch += [pltpu.VMEM(ext, BF16), pltpu.VMEM(ext, BF16),
                    pltpu.VMEM(out_scr, F32), pltpu.VMEM(out_scr, F32)]
    return pl.pallas_call(
        functools.partial(_attention_body, seq_len=seq_len),
        grid=(b, seq_len // tq),
        in_specs=in_specs,
        out_specs=pl.BlockSpec((1, tq, GROUP_WIDTH), lambda bi, ti: (bi, ti, 0)),
        out_shape=jax.ShapeDtypeStruct((b, seq_len, GROUP_WIDTH), BF16),
        scratch_shapes=scratch,
        compiler_params=_params(2),
        name="attention",
    )(*args)


def _mixer_out_body(x_ref, oa_ref, sa_ref, mb_ref, wpa_ref, wout_ref, out_ref):
    y_a = jnp.dot(oa_ref[0], wpa_ref[...], preferred_element_type=F32)
    merged = sa_ref[0].astype(F32) * y_a + mb_ref[0].astype(F32)
    out_ref[0] = x_ref[0] + jnp.dot(merged.astype(BF16), wout_ref[...], preferred_element_type=F32)


def _mixer_out(x, oa, sa, mb, wpa, wout):
    b, s, d_model = x.shape
    tm = min(ROW_TILE, s)
    assert s % tm == 0

    def rows(width):
        return pl.BlockSpec((1, tm, width), lambda bi, ti: (bi, ti, 0))

    return pl.pallas_call(
        _mixer_out_body,
        grid=(b, s // tm),
        in_specs=[rows(d_model), rows(oa.shape[2]), rows(d_model), rows(d_model),
                  _const_spec(wpa.shape), _const_spec(wout.shape)],
        out_specs=rows(d_model),
        out_shape=jax.ShapeDtypeStruct(x.shape, F32),
        compiler_params=_params(2),
        name="mixer_out",
    )(x, oa, sa, mb, wpa, wout)


def _conv_ffn_body(x_ref, xp_ref, xn_ref, nf_ref, wup_ref, cw_ref, cb_ref, wdown_ref, nfin_ref,
                   out_ref, *, final_norm):
    ti = pl.program_id(1)
    n_tiles = pl.num_programs(1)
    tm = x_ref.shape[1]
    d_ff = wdown_ref.shape[0]
    x = x_ref[0]
    x_ext = jnp.concatenate([xp_ref[0], x, xn_ref[0]], axis=0)
    h = _rmsnorm(x_ext, nf_ref[...])
    row = lax.broadcasted_iota(jnp.int32, (x_ext.shape[0], 1), 0)
    outside = ((ti == 0) & (row < HALO)) | ((ti == n_tiles - 1) & (row >= HALO + tm))
    h = jnp.where(outside, 0.0, h).astype(BF16)

    n_ext = x_ext.shape[0]
    acc = jnp.zeros((tm, x.shape[1]), F32)
    for c in range(d_ff // FF_CHUNK):
        halves = []
        for base in (0, d_ff):
            cols = slice(base + c * FF_CHUNK, base + (c + 1) * FF_CHUNK)
            a = jnp.dot(h, wup_ref[:, cols], preferred_element_type=F32)
            cw = cw_ref[:, cols]
            conv = (pltpu.roll(a, 1, 0) * cw[0:1] + a * cw[1:2] + pltpu.roll(a, n_ext - 1, 0) * cw[2:3]
                    + cb_ref[:, cols])
            halves.append(conv[HALO:HALO + tm])
        act = (_gelu_tanh(halves[0]) * halves[1]).astype(BF16)
        acc = acc + jnp.dot(act, wdown_ref[c * FF_CHUNK:(c + 1) * FF_CHUNK, :], preferred_element_type=F32)
    y = x + acc
    if final_norm:
        y = _rmsnorm(y, nfin_ref[...])
    out_ref[0] = y


def _conv_ffn(x, nf, wup, cw, cb, wdown, nfin, final_norm):
    b, s, d_model = x.shape
    tm = min(ROW_TILE, s)
    assert s % tm == 0 and wdown.shape[0] % FF_CHUNK == 0
    halo_per_tile = tm // HALO
    n_halo = s // HALO
    return pl.pallas_call(
        functools.partial(_conv_ffn_body, final_norm=final_norm),
        grid=(b, s // tm),
        in_specs=[
            pl.BlockSpec((1, tm, d_model), lambda bi, ti: (bi, ti, 0)),
            pl.BlockSpec((1, HALO, d_model), lambda bi, ti: (bi, jnp.maximum(ti * halo_per_tile - 1, 0), 0)),
            pl.BlockSpec((1, HALO, d_model),
                         lambda bi, ti: (bi, jnp.minimum((ti + 1) * halo_per_tile, n_halo - 1), 0)),
            _const_spec(nf.shape), _const_spec(wup.shape), _const_spec(cw.shape), _const_spec(cb.shape),
            _const_spec(wdown.shape), _const_spec(nfin.shape),
        ],
        out_specs=pl.BlockSpec((1, tm, d_model), lambda bi, ti: (bi, ti, 0)),
        out_shape=jax.ShapeDtypeStruct(x.shape, F32),
        compiler_params=_params(2),
        name="conv_ffn",
    )(x, x, x, nf, wup, cw, cb, wdown, nfin)


def _rel_bucket(rel):
    nb = NUM_BUCKETS // 2
    max_exact = nb // 2
    ret = np.where(rel > 0, nb, 0)
    n = np.abs(rel)
    nf = np.maximum(n, 1).astype(np.float32)
    large = max_exact + (np.log(nf / max_exact) / math.log(MAX_DISTANCE / max_exact)
                         * (nb - max_exact)).astype(np.int32)
    large = np.minimum(large, nb - 1)
    return (ret + np.where(n < max_exact, n, large)).astype(np.int32)


def _score_bias(rel_bias):
    rel = np.arange(KEY_BLOCK)[None, :] - HALF_WINDOW - np.arange(Q_BLOCK)[:, None]
    band = jnp.asarray(np.abs(rel) <= HALF_WINDOW)
    out = []
    for g, (_, dil) in enumerate(DILATED_GROUPS):
        tab = rel_bias[:, g * HEADS_PER_GROUP:(g + 1) * HEADS_PER_GROUP].astype(F32)
        bias = jnp.transpose(tab[jnp.asarray(_rel_bucket(rel * dil))], (2, 0, 1))
        bias = jnp.where(band[None], bias, MASK_VALUE)
        out.append(bias.reshape(HEADS_PER_GROUP // 2, 2 * Q_BLOCK, KEY_BLOCK))
    return out


def _layer_params(l, norm_mix, w_in, v_gain, w_s, b_s, w_proj_a, w_proj_b, w_out, norm_ffn, w_up,
                  conv_w, conv_b, w_down):
    attn_w = N_GROUPS * GROUP_WIDTH
    width = v_gain.shape[1]
    w = w_in[l].astype(BF16)
    wq = [jnp.concatenate([w[:, part * attn_w + g * GROUP_WIDTH: part * attn_w + (g + 1) * GROUP_WIDTH]
                           for part in range(3)], axis=1) for g in range(N_GROUPS)]
    wuv = w[:, 3 * attn_w: 3 * attn_w + 2 * width]
    wg = w[:, 3 * attn_w + 2 * width:]
    zb = jnp.repeat(b_s[l].T.astype(F32), width // SG_GROUPS, axis=1)
    return dict(
        nm=norm_mix[l][None, :], wq=wq, wuv=wuv, wg=wg, vgain=v_gain[l][None, :],
        ws=w_s[l].astype(BF16), zb=zb, wpb=w_proj_b[l].astype(BF16),
        wpa=w_proj_a[l].astype(BF16), wout=w_out[l].astype(BF16),
        nf=norm_ffn[l][None, :], wup=w_up[l].astype(BF16), cw=conv_w[l], cb=conv_b[l][None, :],
        wdown=w_down[l].astype(BF16),
    )


def _trunk(x, layers, biases, nfin):
    seq_len = x.shape[1]
    for i, p in enumerate(layers):
        *qkvs, sa, mb = _mixer_in(x, p["nm"], p["wq"], p["wuv"], p["wg"], p["vgain"], p["ws"], p["zb"],
                                  p["wpb"])
        oa = _attention(qkvs, biases, seq_len)
        x = _mixer_out(x, oa, sa, mb, p["wpa"], p["wout"])
        x = _conv_ffn(x, p["nf"], p["wup"], p["cw"], p["cb"], p["wdown"], nfin,
                      final_norm=(i == len(layers) - 1))
    return x


def kernel(x_prompt, x_sample, rel_bias, norm_mix, w_in, v_gain, w_s, b_s, w_proj_a, w_proj_b, w_out,
           norm_ffn, w_up, conv_w, conv_b, w_down, norm_final):
    depth = w_in.shape[0]
    layers = [_layer_params(l, norm_mix, w_in, v_gain, w_s, b_s, w_proj_a, w_proj_b, w_out, norm_ffn,
                            w_up, conv_w, conv_b, w_down) for l in range(depth)]
    biases = _score_bias(rel_bias)
    nfin = norm_final[None, :]
    return (_trunk(x_prompt, layers, biases, nfin), _trunk(x_sample, layers, biases, nfin))
```

```python
import functools
import math

import jax
import jax.numpy as jnp
import numpy as np
from jax import lax
from jax.experimental import pallas as pl
from jax.experimental.pallas import tpu as pltpu

F32 = jnp.float32
BF16 = jnp.bfloat16

HEAD_DIM = 64
HEADS_PER_GROUP = 4
GROUP_WIDTH = HEADS_PER_GROUP * HEAD_DIM
DILATED_GROUPS = ((128, 1), (512, 4), (2048, 16))
N_GROUPS = len(DILATED_GROUPS)
HALF_WINDOW = 64
NUM_BUCKETS = 32
MAX_DISTANCE = 1024
SG_CHUNK = 128
SG_GROUPS = 8
CONV_WIDTH = 3
EPS = 1e-6
MASK_VALUE = -1e30
LOG2E = math.log2(math.e)

LANES = 128
SUBLANES = 8
BF16_ROWS = 16
VMEM_LIMIT_BYTES = 56 * 1024 * 1024

ROW_TILE = 512
ATTN_TILE = 2048
Q_BLOCK = 128
KEY_BLOCK = Q_BLOCK + 2 * HALF_WINDOW
ATTN_UNROLL = 4
FF_CHUNK = 2 * LANES
MERGE_ROWS = 256


def _rmsnorm(x, gain):
    y = x * lax.rsqrt(jnp.mean(x * x, axis=-1, keepdims=True) + EPS)
    return y * gain


def _gelu_tanh(x):
    c = math.sqrt(2.0 / math.pi)
    return 0.5 * x * (1.0 + jnp.tanh(c * (x + 0.044715 * (x * x * x))))


def _sigmoid(x):
    return 1.0 / (1.0 + jnp.exp(-x))


def _const_spec(shape):
    zeros = (0,) * len(shape)
    return pl.BlockSpec(shape, lambda *_: zeros, pipeline_mode=pl.Buffered(1))


def _params(n_grid_axes):
    return pltpu.CompilerParams(
        dimension_semantics=("arbitrary",) * n_grid_axes,
        vmem_limit_bytes=VMEM_LIMIT_BYTES,
    )


def _mixer_in_body(x_ref, nm_ref, wq0_ref, wq1_ref, wq2_ref, wuv_ref, wg_ref, vgain_ref, ws_ref,
                   zb_ref, wpb_ref, qkv0_ref, qkv1_ref, qkv2_ref, sa_ref, mb_ref, perm_ref):
    tm = x_ref.shape[1]
    width = vgain_ref.shape[1]
    h = _rmsnorm(x_ref[0], nm_ref[...]).astype(BF16)

    col = lax.broadcasted_iota(jnp.int32, (1, 3 * GROUP_WIDTH), 1)
    qscale = jnp.where(col < GROUP_WIDTH, LOG2E * HEAD_DIM ** -0.5, 1.0).astype(F32)
    for (_, dil), w_ref, out_ref in zip(DILATED_GROUPS, (wq0_ref, wq1_ref, wq2_ref),
                                        (qkv0_ref, qkv1_ref, qkv2_ref)):
        r = jnp.dot(h, w_ref[...], preferred_element_type=F32) * qscale
        if dil == 1:
            out_ref[0, 0] = r.astype(BF16)
        else:
            for c in range(perm_ref.shape[0]):
                perm_ref[c] = r[:, c * LANES:(c + 1) * LANES]
            for res in range(dil):
                for c in range(perm_ref.shape[0]):
                    out_ref[0, res, :, c * LANES:(c + 1) * LANES] = (
                        perm_ref[c, pl.ds(res, tm // dil, stride=dil), :].astype(BF16))

    uv = _gelu_tanh(jnp.dot(h, wuv_ref[...], preferred_element_type=F32))
    u = uv[:, :width]
    vn = _rmsnorm(uv[:, width:], vgain_ref[...]).astype(BF16)
    gch = width // SG_GROUPS
    uz = []
    for c in range(tm // SG_CHUNK):
        rows = slice(c * SG_CHUNK, (c + 1) * SG_CHUNK)
        z = jnp.concatenate(
            [jnp.dot(ws_ref[g], vn[rows, g * gch:(g + 1) * gch], preferred_element_type=F32)
             for g in range(SG_GROUPS)], axis=1)
        uz.append((u[rows] * (z + zb_ref[...])).astype(BF16))
    y_b = jnp.dot(jnp.concatenate(uz, axis=0), wpb_ref[...], preferred_element_type=F32)

    gates = jnp.dot(h, wg_ref[...], preferred_element_type=F32)
    d_model = gates.shape[1] // 2
    sa_ref[0] = _sigmoid(gates[:, :d_model]).astype(BF16)
    mb_ref[0] = (_sigmoid(gates[:, d_model:]) * y_b).astype(BF16)


def _mixer_in(x, nm, wq, wuv, wg, vgain, ws, zb, wpb):
    b, s, d_model = x.shape
    tm = min(ROW_TILE, s)
    assert s % tm == 0 and tm % SG_CHUNK == 0
    max_dil = max(d for _, d in DILATED_GROUPS)
    assert tm % (max_dil * BF16_ROWS) == 0
    qkv_w = 3 * GROUP_WIDTH
    out_shape = [jax.ShapeDtypeStruct((b, dil, s // dil, qkv_w), BF16) for _, dil in DILATED_GROUPS]
    out_shape += [jax.ShapeDtypeStruct((b, s, d_model), BF16)] * 2
    out_specs = [pl.BlockSpec((1, dil, tm // dil, qkv_w), lambda bi, ti: (bi, 0, ti, 0))
                 for _, dil in DILATED_GROUPS]
    out_specs += [pl.BlockSpec((1, tm, d_model), lambda bi, ti: (bi, ti, 0))] * 2
    in_specs = [pl.BlockSpec((1, tm, d_model), lambda bi, ti: (bi, ti, 0)), _const_spec(nm.shape)]
    in_specs += [_const_spec(w.shape) for w in wq]
    in_specs += [_const_spec(a.shape) for a in (wuv, wg, vgain, ws, zb, wpb)]
    return pl.pallas_call(
        _mixer_in_body,
        grid=(b, s // tm),
        in_specs=in_specs,
        out_specs=out_specs,
        out_shape=out_shape,
        scratch_shapes=[pltpu.VMEM((qkv_w // LANES, tm, LANES), F32)],
        compiler_params=_params(2),
        name="mixer_in",
    )(x, nm, *wq, wuv, wg, vgain, ws, zb, wpb)


def _attend_pair(q_pair, k_pair, v_pair, bias, low_half):
    zero = jnp.zeros_like(q_pair)
    q_stack = jnp.concatenate([jnp.where(low_half, q_pair, zero), jnp.where(low_half, zero, q_pair)],
                              axis=0)
    s = lax.dot_general(q_stack, k_pair, (((1,), (1,)), ((), ())), preferred_element_type=F32)
    s = s + bias
    m = jnp.max(s, axis=-1, keepdims=True)
    p = jnp.exp2(s - m)
    denom = jnp.sum(p, axis=-1, keepdims=True)
    pv = jnp.dot(p.astype(BF16), v_pair, preferred_element_type=F32)
    inv = 1.0 / denom
    lse = m + jnp.log2(denom)
    out = jnp.where(low_half, pv[:Q_BLOCK] * inv[:Q_BLOCK], pv[Q_BLOCK:] * inv[Q_BLOCK:])
    return out, jnp.where(low_half, lse[:Q_BLOCK], lse[Q_BLOCK:])


def _attention_body(*refs):
    n_in = 8 * N_GROUPS
    in_refs, out_ref, scratch = refs[:n_in], refs[n_in], refs[n_in + 1:]
    ti = pl.program_id(1)
    n_tiles = pl.num_programs(1)
    tq = out_ref.shape[1]
    lane = lax.broadcasted_iota(jnp.int32, (1, LANES), 1)
    low_half = lane < HEAD_DIM

    for g, (_, dil) in enumerate(DILATED_GROUPS):
        (q_ref, k_ref, kp_ref, kn_ref, v_ref, vp_ref, vn_ref, bias_ref) = in_refs[8 * g:8 * g + 8]
        kx_ref, vx_ref, o_ref, lse_ref = scratch[4 * g:4 * g + 4]
        tl = tq // dil
        n_blocks = tl // Q_BLOCK

        kx_ref[:, :HALF_WINDOW, :] = kp_ref[0]
        kx_ref[:, HALF_WINDOW:HALF_WINDOW + tl, :] = k_ref[0]
        kx_ref[:, HALF_WINDOW + tl:, :] = kn_ref[0]
        vx_ref[:, :HALF_WINDOW, :] = vp_ref[0]
        vx_ref[:, HALF_WINDOW:HALF_WINDOW + tl, :] = v_ref[0]
        vx_ref[:, HALF_WINDOW + tl:, :] = vn_ref[0]

        def blocks(step, carry, *, dil=dil, n_blocks=n_blocks, q_ref=q_ref, kx_ref=kx_ref,
                   vx_ref=vx_ref, bias_ref=bias_ref, o_ref=o_ref, lse_ref=lse_ref):
            for u in range(ATTN_UNROLL):
                it = step * ATTN_UNROLL + u
                res = it // n_blocks
                jb = it % n_blocks
                row0 = pl.multiple_of(jb * Q_BLOCK, Q_BLOCK)
                q_blk = q_ref[0, res, pl.ds(row0, Q_BLOCK), :]
                k_win = kx_ref[res, pl.ds(row0, KEY_BLOCK), :]
                v_win = vx_ref[res, pl.ds(row0, KEY_BLOCK), :]
                at_start = jnp.logical_and(ti == 0, jb == 0)
                at_end = jnp.logical_and(ti == n_tiles - 1, jb == n_blocks - 1)
                variant = at_start.astype(jnp.int32) + 2 * at_end.astype(jnp.int32)
                outs, lses = [], []
                for hp in range(HEADS_PER_GROUP // 2):
                    cols = slice(hp * LANES, (hp + 1) * LANES)
                    o_pair, lse_pair = _attend_pair(q_blk[:, cols], k_win[:, cols], v_win[:, cols],
                                                    bias_ref[variant, hp], low_half)
                    outs.append(o_pair)
                    lses.append(lse_pair)
                if dil == 1:
                    o_ref[pl.ds(row0, Q_BLOCK), :] = jnp.concatenate(outs, axis=1)
                    lse_ref[pl.ds(row0, Q_BLOCK), :] = jnp.concatenate(lses, axis=1)
                else:
                    start = row0 * dil + res
                    for c in range(GROUP_WIDTH // LANES):
                        o_ref[c, pl.ds(start, Q_BLOCK, stride=dil), :] = outs[c]
                        lse_ref[c, pl.ds(start, Q_BLOCK, stride=dil), :] = lses[c]
            return carry

        lax.fori_loop(0, dil * n_blocks // ATTN_UNROLL, blocks, 0)

    o0_ref, l0_ref = scratch[2], scratch[3]
    o1_ref, l1_ref = scratch[6], scratch[7]
    o2_ref, l2_ref = scratch[10], scratch[11]

    def merge(it, carry):
        row0 = pl.multiple_of(it * MERGE_ROWS, MERGE_ROWS)
        rows = pl.ds(row0, MERGE_ROWS)
        for c in range(GROUP_WIDTH // LANES):
            cols = slice(c * LANES, (c + 1) * LANES)
            l0, l1, l2 = l0_ref[rows, cols], l1_ref[c, rows, :], l2_ref[c, rows, :]
            mx = jnp.maximum(jnp.maximum(l0, l1), l2)
            w0, w1, w2 = jnp.exp2(l0 - mx), jnp.exp2(l1 - mx), jnp.exp2(l2 - mx)
            num = w0 * o0_ref[rows, cols] + w1 * o1_ref[c, rows, :] + w2 * o2_ref[c, rows, :]
            out_ref[0, rows, cols] = (num / (w0 + w1 + w2)).astype(BF16)
        return carry

    lax.fori_loop(0, tq // MERGE_ROWS, merge, 0)


def _attention(qkvs, biases, seq_len):
    b = qkvs[0].shape[0]
    tq = min(ATTN_TILE, seq_len)
    assert seq_len % tq == 0
    in_specs, args, scratch = [], [], []
    for (_, dil), qkv, bias in zip(DILATED_GROUPS, qkvs, biases):
        tl = tq // dil
        assert tl % Q_BLOCK == 0 and (tq // Q_BLOCK) % ATTN_UNROLL == 0
        halo_per_tile = tl // HALF_WINDOW
        n_halo = seq_len // dil // HALF_WINDOW

        def main(col):
            return pl.BlockSpec((1, dil, tl, GROUP_WIDTH), lambda bi, ti, col=col: (bi, 0, ti, col))

        def before(col, hpt=halo_per_tile):
            return pl.BlockSpec((1, dil, HALF_WINDOW, GROUP_WIDTH),
                                lambda bi, ti, col=col: (bi, 0, jnp.maximum(ti * hpt - 1, 0), col))

        def after(col, hpt=halo_per_tile, n_halo=n_halo):
            return pl.BlockSpec((1, dil, HALF_WINDOW, GROUP_WIDTH),
                                lambda bi, ti, col=col: (bi, 0, jnp.minimum((ti + 1) * hpt, n_halo - 1), col))

        in_specs += [main(0), main(1), before(1), after(1), main(2), before(2), after(2),
                     _const_spec(bias.shape)]
        args += [qkv] * 7 + [bias]
        ext = (dil, tl + 2 * HALF_WINDOW, GROUP_WIDTH)
        if dil == 1:
            out_scr = (tq, GROUP_WIDTH)
        else:
            out_scr = (GROUP_WIDTH // LANES, tq, LANES)
        scratch += [pltpu.VMEM(ext, BF16), pltpu.VMEM(ext, BF16),
                    pltpu.VMEM(out_scr, F32), pltpu.VMEM(out_scr, F32)]
    return pl.pallas_call(
        _attention_body,
        grid=(b, seq_len // tq),
        in_specs=in_specs,
        out_specs=pl.BlockSpec((1, tq, GROUP_WIDTH), lambda bi, ti: (bi, ti, 0)),
        out_shape=jax.ShapeDtypeStruct((b, seq_len, GROUP_WIDTH), BF16),
        scratch_shapes=scratch,
        compiler_params=_params(2),
        name="attention",
    )(*args)


def _mixer_out_body(x_ref, oa_ref, sa_ref, mb_ref, wpa_ref, wout_ref, out_ref):
    y_a = jnp.dot(oa_ref[0], wpa_ref[...], preferred_element_type=F32)
    merged = sa_ref[0].astype(F32) * y_a + mb_ref[0].astype(F32)
    out_ref[0] = x_ref[0] + jnp.dot(merged.astype(BF16), wout_ref[...], preferred_element_type=F32)


def _mixer_out(x, oa, sa, mb, wpa, wout):
    b, s, d_model = x.shape
    tm = min(ROW_TILE, s)
    assert s % tm == 0

    def rows(width):
        return pl.BlockSpec((1, tm, width), lambda bi, ti: (bi, ti, 0))

    return pl.pallas_call(
        _mixer_out_body,
        grid=(b, s // tm),
        in_specs=[rows(d_model), rows(oa.shape[2]), rows(d_model), rows(d_model),
                  _const_spec(wpa.shape), _const_spec(wout.shape)],
        out_specs=rows(d_model),
        out_shape=jax.ShapeDtypeStruct(x.shape, F32),
        compiler_params=_params(2),
        name="mixer_out",
    )(x, oa, sa, mb, wpa, wout)


def _ffn_halo_body(x_ref, nf_ref, wup_ref, out_ref):
    h = _rmsnorm(x_ref[...], nf_ref[...]).astype(BF16)
    out_ref[...] = jnp.dot(h, wup_ref[...], preferred_element_type=F32)


def _ffn_halo(x, nf, wup, tm):
    b, s, d_model = x.shape
    nt = s // tm
    xt = x.reshape(b, nt, tm, d_model)
    zero = jnp.zeros((b, 1, d_model), x.dtype)
    before = jnp.concatenate([zero, xt[:, :-1, tm - 1, :]], axis=1)
    after = jnp.concatenate([xt[:, 1:, 0, :], zero], axis=1)
    rows = jnp.stack([before, after], axis=2).reshape(b * nt * 2, d_model)
    a = pl.pallas_call(
        _ffn_halo_body,
        out_shape=jax.ShapeDtypeStruct((rows.shape[0], wup.shape[1]), F32),
        compiler_params=pltpu.CompilerParams(vmem_limit_bytes=VMEM_LIMIT_BYTES),
        name="ffn_halo",
    )(rows, nf, wup)
    return a.reshape(b, nt, 2, wup.shape[1])


def _conv_ffn_body(x_ref, halo_ref, nf_ref, wup_ref, cw_ref, cb_ref, wdown_ref, nfin_ref, out_ref,
                   act_ref, *, final_norm):
    tm = x_ref.shape[1]
    x = x_ref[0]
    h = _rmsnorm(x, nf_ref[...]).astype(BF16)
    first_row = lax.broadcasted_iota(jnp.int32, (SUBLANES, 1), 0) == 0
    last_row = lax.broadcasted_iota(jnp.int32, (SUBLANES, 1), 0) == SUBLANES - 1
    half = FF_CHUNK // 2
    for c in range(wup_ref.shape[1] // FF_CHUNK):
        cols = slice(c * FF_CHUNK, (c + 1) * FF_CHUNK)
        a = jnp.dot(h, wup_ref[:, cols], preferred_element_type=F32)
        halo = halo_ref[0, 0, :, cols]
        prev = pltpu.roll(a, 1, 0)
        prev = jnp.concatenate([jnp.where(first_row, halo[0:1], prev[:SUBLANES]), prev[SUBLANES:]], axis=0)
        nxt = pltpu.roll(a, tm - 1, 0)
        nxt = jnp.concatenate([nxt[:-SUBLANES], jnp.where(last_row, halo[1:2], nxt[-SUBLANES:])], axis=0)
        cw = cw_ref[:, cols]
        conv = prev * cw[0:1] + a * cw[1:2] + nxt * cw[2:3] + cb_ref[:, cols]
        act_ref[:, c * half:(c + 1) * half] = (_gelu_tanh(conv[:, :half]) * conv[:, half:]).astype(BF16)
    y = x + jnp.dot(act_ref[...], wdown_ref[...], preferred_element_type=F32)
    if final_norm:
        y = _rmsnorm(y, nfin_ref[...])
    out_ref[0] = y


def _conv_ffn(x, nf, wup, cw, cb, wdown, nfin, final_norm):
    b, s, d_model = x.shape
    tm = min(ROW_TILE, s)
    d_ff = wdown.shape[0]
    assert s % tm == 0 and (2 * d_ff) % FF_CHUNK == 0
    halo = _ffn_halo(x, nf, wup, tm)
    return pl.pallas_call(
        functools.partial(_conv_ffn_body, final_norm=final_norm),
        grid=(b, s // tm),
        in_specs=[
            pl.BlockSpec((1, tm, d_model), lambda bi, ti: (bi, ti, 0)),
            pl.BlockSpec((1, 1, 2, 2 * d_ff), lambda bi, ti: (bi, ti, 0, 0)),
            _const_spec(nf.shape), _const_spec(wup.shape), _const_spec(cw.shape), _const_spec(cb.shape),
            _const_spec(wdown.shape), _const_spec(nfin.shape),
        ],
        out_specs=pl.BlockSpec((1, tm, d_model), lambda bi, ti: (bi, ti, 0)),
        out_shape=jax.ShapeDtypeStruct(x.shape, F32),
        scratch_shapes=[pltpu.VMEM((tm, d_ff), BF16)],
        compiler_params=_params(2),
        name="conv_ffn",
    )(x, halo, nf, wup, cw, cb, wdown, nfin)


def _rel_bucket(rel):
    nb = NUM_BUCKETS // 2
    max_exact = nb // 2
    ret = np.where(rel > 0, nb, 0)
    n = np.abs(rel)
    nf = np.maximum(n, 1).astype(np.float32)
    large = max_exact + (np.log(nf / max_exact) / math.log(MAX_DISTANCE / max_exact)
                         * (nb - max_exact)).astype(np.int32)
    large = np.minimum(large, nb - 1)
    return (ret + np.where(n < max_exact, n, large)).astype(np.int32)


def _score_bias(rel_bias):
    period = Q_BLOCK + KEY_BLOCK
    rel = np.arange(period) - (Q_BLOCK - 1) - HALF_WINDOW
    in_band = jnp.asarray(np.abs(rel) <= HALF_WINDOW)
    col = np.arange(KEY_BLOCK)
    col_masks = [np.zeros(KEY_BLOCK, bool), col < HALF_WINDOW, col >= Q_BLOCK + HALF_WINDOW]
    col_masks.append(col_masks[1] | col_masks[2])
    out = []
    for g, (_, dil) in enumerate(DILATED_GROUPS):
        tab = rel_bias[:, g * HEADS_PER_GROUP:(g + 1) * HEADS_PER_GROUP].astype(F32)
        by_offset = jnp.where(in_band[:, None], tab[jnp.asarray(_rel_bucket(rel * dil))] * LOG2E, MASK_VALUE)
        by_offset = jnp.roll(by_offset.T, -(Q_BLOCK - 1), axis=1)
        bias = jnp.tile(by_offset, (1, Q_BLOCK))[:, :Q_BLOCK * (period - 1)]
        bias = bias.reshape(HEADS_PER_GROUP, Q_BLOCK, period - 1)[:, :, :KEY_BLOCK]
        bias = bias.reshape(HEADS_PER_GROUP // 2, 2 * Q_BLOCK, KEY_BLOCK)
        out.append(jnp.stack([jnp.where(jnp.asarray(cm), MASK_VALUE, bias) for cm in col_masks]))
    return out


def _layer_params(l, norm_mix, w_in, v_gain, w_s, b_s, w_proj_a, w_proj_b, w_out, norm_ffn, w_up,
                  conv_w, conv_b, w_down):
    attn_w = N_GROUPS * GROUP_WIDTH
    width = v_gain.shape[1]
    d_ff = w_down.shape[1]
    w = w_in[l].astype(BF16)
    wq = [jnp.concatenate([w[:, part * attn_w + g * GROUP_WIDTH: part * attn_w + (g + 1) * GROUP_WIDTH]
                           for part in range(3)], axis=1) for g in range(N_GROUPS)]
    wuv = w[:, 3 * attn_w: 3 * attn_w + 2 * width]
    wg = w[:, 3 * attn_w + 2 * width:]
    zb = jnp.repeat(b_s[l].T.astype(F32), width // SG_GROUPS, axis=1)

    def pair_columns(a):
        half = FF_CHUNK // 2
        lead = a.shape[:-1]
        return a.reshape(lead + (2, d_ff // half, half)).swapaxes(-3, -2).reshape(lead + (2 * d_ff,))

    return dict(
        nm=norm_mix[l][None, :], wq=wq, wuv=wuv, wg=wg, vgain=v_gain[l][None, :],
        ws=w_s[l].astype(BF16), zb=zb, wpb=w_proj_b[l].astype(BF16),
        wpa=w_proj_a[l].astype(BF16), wout=w_out[l].astype(BF16),
        nf=norm_ffn[l][None, :], wup=pair_columns(w_up[l].astype(BF16)), cw=pair_columns(conv_w[l]),
        cb=pair_columns(conv_b[l][None, :]), wdown=w_down[l].astype(BF16),
    )


def _trunk(x, layers, biases, nfin):
    seq_len = x.shape[1]
    for i, p in enumerate(layers):
        *qkvs, sa, mb = _mixer_in(x, p["nm"], p["wq"], p["wuv"], p["wg"], p["vgain"], p["ws"], p["zb"],
                                  p["wpb"])
        oa = _attention(qkvs, biases, seq_len)
        x = _mixer_out(x, oa, sa, mb, p["wpa"], p["wout"])
        x = _conv_ffn(x, p["nf"], p["wup"], p["cw"], p["cb"], p["wdown"], nfin,
                      final_norm=(i == len(layers) - 1))
    return x


def kernel(x_prompt, x_sample, rel_bias, norm_mix, w_in, v_gain, w_s, b_s, w_proj_a, w_proj_b, w_out,
           norm_ffn, w_up, conv_w, conv_b, w_down, norm_final):
    depth = w_in.shape[0]
    layers = [_layer_params(l, norm_mix, w_in, v_gain, w_s, b_s, w_proj_a, w_proj_b, w_out, norm_ffn,
                            w_up, conv_w, conv_b, w_down) for l in range(depth)]
    biases = _score_bias(rel_bias)
    nfin = norm_final[None, :]
    return (_trunk(x_prompt, layers, biases, nfin), _trunk(x_sample, layers, biases, nfin))
```

```python
import functools
import math

import jax
import jax.numpy as jnp
import numpy as np
from jax import lax
from jax.experimental import pallas as pl
from jax.experimental.pallas import tpu as pltpu

F32 = jnp.float32
BF16 = jnp.bfloat16

HEAD_DIM = 64
HEADS_PER_GROUP = 4
GROUP_WIDTH = HEADS_PER_GROUP * HEAD_DIM
DILATED_GROUPS = ((128, 1), (512, 4), (2048, 16))
N_GROUPS = len(DILATED_GROUPS)
HALF_WINDOW = 64
NUM_BUCKETS = 32
MAX_DISTANCE = 1024
SG_CHUNK = 128
SG_GROUPS = 8
CONV_WIDTH = 3
EPS = 1e-6
MASK_VALUE = -1e30
LOG2E = math.log2(math.e)

LANES = 128
SUBLANES = 8
BF16_ROWS = 16
VMEM_LIMIT_BYTES = 56 * 1024 * 1024

ROW_TILE = 512
ATTN_TILE = 2048
Q_BLOCK = 128
KEY_BLOCK = Q_BLOCK + 2 * HALF_WINDOW
ATTN_UNROLL = 4
FF_CHUNK = 2 * LANES
MERGE_ROWS = 256


def _rmsnorm(x, gain):
    y = x * lax.rsqrt(jnp.mean(x * x, axis=-1, keepdims=True) + EPS)
    return y * gain


def _gelu_tanh(x):
    c = math.sqrt(2.0 / math.pi)
    return 0.5 * x * (1.0 + jnp.tanh(c * (x + 0.044715 * (x * x * x))))


def _gelu_tanh_exp(x):
    k0 = -2.0 * math.sqrt(2.0 / math.pi) * LOG2E
    return x / (1.0 + jnp.exp2(x * (k0 + (0.044715 * k0) * (x * x))))


def _sigmoid(x):
    return 1.0 / (1.0 + jnp.exp(-x))


def _const_spec(shape):
    zeros = (0,) * len(shape)
    return pl.BlockSpec(shape, lambda *_: zeros, pipeline_mode=pl.Buffered(1))


def _params(n_grid_axes):
    return pltpu.CompilerParams(
        dimension_semantics=("arbitrary",) * n_grid_axes,
        vmem_limit_bytes=VMEM_LIMIT_BYTES,
    )


def _mixer_in_body(x_ref, nm_ref, wq0_ref, wq1_ref, wq2_ref, wuv_ref, wg_ref, vgain_ref, ws_ref,
                   zb_ref, wpb_ref, qkv0_ref, qkv1_ref, qkv2_ref, sa_ref, mb_ref, perm_ref):
    tm = x_ref.shape[1]
    width = vgain_ref.shape[1]
    h = _rmsnorm(x_ref[0], nm_ref[...]).astype(BF16)

    col = lax.broadcasted_iota(jnp.int32, (1, 3 * GROUP_WIDTH), 1)
    qscale = jnp.where(col < GROUP_WIDTH, LOG2E * HEAD_DIM ** -0.5, 1.0).astype(F32)
    for (_, dil), w_ref, out_ref in zip(DILATED_GROUPS, (wq0_ref, wq1_ref, wq2_ref),
                                        (qkv0_ref, qkv1_ref, qkv2_ref)):
        r = jnp.dot(h, w_ref[...], preferred_element_type=F32) * qscale
        if dil == 1:
            out_ref[0, 0] = r.astype(BF16)
        else:
            for c in range(perm_ref.shape[0]):
                perm_ref[c] = r[:, c * LANES:(c + 1) * LANES]
            for res in range(dil):
                for c in range(perm_ref.shape[0]):
                    out_ref[0, res, :, c * LANES:(c + 1) * LANES] = (
                        perm_ref[c, pl.ds(res, tm // dil, stride=dil), :].astype(BF16))

    uv = _gelu_tanh(jnp.dot(h, wuv_ref[...], preferred_element_type=F32))
    u = uv[:, :width]
    vn = _rmsnorm(uv[:, width:], vgain_ref[...]).astype(BF16)
    gch = width // SG_GROUPS
    uz = []
    for c in range(tm // SG_CHUNK):
        rows = slice(c * SG_CHUNK, (c + 1) * SG_CHUNK)
        z = jnp.concatenate(
            [jnp.dot(ws_ref[g], vn[rows, g * gch:(g + 1) * gch], preferred_element_type=F32)
             for g in range(SG_GROUPS)], axis=1)
        uz.append((u[rows] * (z + zb_ref[...])).astype(BF16))
    y_b = jnp.dot(jnp.concatenate(uz, axis=0), wpb_ref[...], preferred_element_type=F32)

    gates = jnp.dot(h, wg_ref[...], preferred_element_type=F32)
    d_model = gates.shape[1] // 2
    sa_ref[0] = _sigmoid(gates[:, :d_model]).astype(BF16)
    mb_ref[0] = (_sigmoid(gates[:, d_model:]) * y_b).astype(BF16)


def _mixer_in(x, nm, wq, wuv, wg, vgain, ws, zb, wpb):
    b, s, d_model = x.shape
    tm = min(ROW_TILE, s)
    assert s % tm == 0 and tm % SG_CHUNK == 0
    max_dil = max(d for _, d in DILATED_GROUPS)
    assert tm % (max_dil * BF16_ROWS) == 0
    qkv_w = 3 * GROUP_WIDTH
    out_shape = [jax.ShapeDtypeStruct((b, dil, s // dil, qkv_w), BF16) for _, dil in DILATED_GROUPS]
    out_shape += [jax.ShapeDtypeStruct((b, s, d_model), BF16)] * 2
    out_specs = [pl.BlockSpec((1, dil, tm // dil, qkv_w), lambda bi, ti: (bi, 0, ti, 0))
                 for _, dil in DILATED_GROUPS]
    out_specs += [pl.BlockSpec((1, tm, d_model), lambda bi, ti: (bi, ti, 0))] * 2
    in_specs = [pl.BlockSpec((1, tm, d_model), lambda bi, ti: (bi, ti, 0)), _const_spec(nm.shape)]
    in_specs += [_const_spec(w.shape) for w in wq]
    in_specs += [_const_spec(a.shape) for a in (wuv, wg, vgain, ws, zb, wpb)]
    return pl.pallas_call(
        _mixer_in_body,
        grid=(b, s // tm),
        in_specs=in_specs,
        out_specs=out_specs,
        out_shape=out_shape,
        scratch_shapes=[pltpu.VMEM((qkv_w // LANES, tm, LANES), F32)],
        compiler_params=_params(2),
        name="mixer_in",
    )(x, nm, *wq, wuv, wg, vgain, ws, zb, wpb)


def _attend_pair(q_pair, k_pair, v_pair, bias, low_half):
    zero = jnp.zeros_like(q_pair)
    q_stack = jnp.concatenate([jnp.where(low_half, q_pair, zero), jnp.where(low_half, zero, q_pair)],
                              axis=0)
    s = lax.dot_general(q_stack, k_pair, (((1,), (1,)), ((), ())), preferred_element_type=F32)
    s = s + bias
    m = jnp.max(s, axis=-1, keepdims=True)
    p = jnp.exp2(s - m)
    denom = jnp.sum(p, axis=-1, keepdims=True)
    pv = jnp.dot(p.astype(BF16), v_pair, preferred_element_type=F32)
    inv = 1.0 / denom
    lse = m + jnp.log2(denom)
    out = jnp.where(low_half, pv[:Q_BLOCK] * inv[:Q_BLOCK], pv[Q_BLOCK:] * inv[Q_BLOCK:])
    return out, jnp.where(low_half, lse[:Q_BLOCK], lse[Q_BLOCK:])


def _attention_body(*refs):
    n_in = 8 * N_GROUPS
    in_refs, out_ref, scratch = refs[:n_in], refs[n_in], refs[n_in + 1:]
    ti = pl.program_id(1)
    n_tiles = pl.num_programs(1)
    tq = out_ref.shape[1]
    lane = lax.broadcasted_iota(jnp.int32, (1, LANES), 1)
    low_half = lane < HEAD_DIM

    for g, (_, dil) in enumerate(DILATED_GROUPS):
        (q_ref, k_ref, kp_ref, kn_ref, v_ref, vp_ref, vn_ref, bias_ref) = in_refs[8 * g:8 * g + 8]
        kx_ref, vx_ref, o_ref, lse_ref = scratch[4 * g:4 * g + 4]
        tl = tq // dil
        n_blocks = tl // Q_BLOCK

        kx_ref[:, :HALF_WINDOW, :] = kp_ref[0]
        kx_ref[:, HALF_WINDOW:HALF_WINDOW + tl, :] = k_ref[0]
        kx_ref[:, HALF_WINDOW + tl:, :] = kn_ref[0]
        vx_ref[:, :HALF_WINDOW, :] = vp_ref[0]
        vx_ref[:, HALF_WINDOW:HALF_WINDOW + tl, :] = v_ref[0]
        vx_ref[:, HALF_WINDOW + tl:, :] = vn_ref[0]

        def blocks(step, carry, *, dil=dil, n_blocks=n_blocks, q_ref=q_ref, kx_ref=kx_ref,
                   vx_ref=vx_ref, bias_ref=bias_ref, o_ref=o_ref, lse_ref=lse_ref):
            for u in range(ATTN_UNROLL):
                it = step * ATTN_UNROLL + u
                res = it // n_blocks
                jb = it % n_blocks
                row0 = pl.multiple_of(jb * Q_BLOCK, Q_BLOCK)
                q_blk = q_ref[0, res, pl.ds(row0, Q_BLOCK), :]
                k_win = kx_ref[res, pl.ds(row0, KEY_BLOCK), :]
                v_win = vx_ref[res, pl.ds(row0, KEY_BLOCK), :]
                at_start = jnp.logical_and(ti == 0, jb == 0)
                at_end = jnp.logical_and(ti == n_tiles - 1, jb == n_blocks - 1)
                variant = at_start.astype(jnp.int32) + 2 * at_end.astype(jnp.int32)
                outs, lses = [], []
                for hp in range(HEADS_PER_GROUP // 2):
                    cols = slice(hp * LANES, (hp + 1) * LANES)
                    o_pair, lse_pair = _attend_pair(q_blk[:, cols], k_win[:, cols], v_win[:, cols],
                                                    bias_ref[variant, hp], low_half)
                    outs.append(o_pair)
                    lses.append(lse_pair)
                if dil == 1:
                    o_ref[pl.ds(row0, Q_BLOCK), :] = jnp.concatenate(outs, axis=1)
                    lse_ref[pl.ds(row0, Q_BLOCK), :] = jnp.concatenate(lses, axis=1)
                else:
                    start = row0 * dil + res
                    for c in range(GROUP_WIDTH // LANES):
                        o_ref[c, pl.ds(start, Q_BLOCK, stride=dil), :] = outs[c]
                        lse_ref[c, pl.ds(start, Q_BLOCK, stride=dil), :] = lses[c]
            return carry

        lax.fori_loop(0, dil * n_blocks // ATTN_UNROLL, blocks, 0)

    o0_ref, l0_ref = scratch[2], scratch[3]
    o1_ref, l1_ref = scratch[6], scratch[7]
    o2_ref, l2_ref = scratch[10], scratch[11]

    def merge(it, carry):
        row0 = pl.multiple_of(it * MERGE_ROWS, MERGE_ROWS)
        rows = pl.ds(row0, MERGE_ROWS)
        for c in range(GROUP_WIDTH // LANES):
            cols = slice(c * LANES, (c + 1) * LANES)
            l0, l1, l2 = l0_ref[rows, cols], l1_ref[c, rows, :], l2_ref[c, rows, :]
            mx = jnp.maximum(jnp.maximum(l0, l1), l2)
            w0, w1, w2 = jnp.exp2(l0 - mx), jnp.exp2(l1 - mx), jnp.exp2(l2 - mx)
            num = w0 * o0_ref[rows, cols] + w1 * o1_ref[c, rows, :] + w2 * o2_ref[c, rows, :]
            out_ref[0, rows, cols] = (num / (w0 + w1 + w2)).astype(BF16)
        return carry

    lax.fori_loop(0, tq // MERGE_ROWS, merge, 0)


def _attention(qkvs, biases, seq_len):
    b = qkvs[0].shape[0]
    tq = min(ATTN_TILE, seq_len)
    assert seq_len % tq == 0
    in_specs, args, scratch = [], [], []
    for (_, dil), qkv, bias in zip(DILATED_GROUPS, qkvs, biases):
        tl = tq // dil
        assert tl % Q_BLOCK == 0 and (tq // Q_BLOCK) % ATTN_UNROLL == 0
        halo_per_tile = tl // HALF_WINDOW
        n_halo = seq_len // dil // HALF_WINDOW

        def main(col):
            return pl.BlockSpec((1, dil, tl, GROUP_WIDTH), lambda bi, ti, col=col: (bi, 0, ti, col))

        def before(col, hpt=halo_per_tile):
            return pl.BlockSpec((1, dil, HALF_WINDOW, GROUP_WIDTH),
                                lambda bi, ti, col=col: (bi, 0, jnp.maximum(ti * hpt - 1, 0), col))

        def after(col, hpt=halo_per_tile, n_halo=n_halo):
            return pl.BlockSpec((1, dil, HALF_WINDOW, GROUP_WIDTH),
                                lambda bi, ti, col=col: (bi, 0, jnp.minimum((ti + 1) * hpt, n_halo - 1), col))

        in_specs += [main(0), main(1), before(1), after(1), main(2), before(2), after(2),
                     _const_spec(bias.shape)]
        args += [qkv] * 7 + [bias]
        ext = (dil, tl + 2 * HALF_WINDOW, GROUP_WIDTH)
        if dil == 1:
            out_scr = (tq, GROUP_WIDTH)
        else:
            out_scr = (GROUP_WIDTH // LANES, tq, LANES)
        scratch += [pltpu.VMEM(ext, BF16), pltpu.VMEM(ext, BF16),
                    pltpu.VMEM(out_scr, F32), pltpu.VMEM(out_scr, F32)]
    return pl.pallas_call(
        _attention_body,
        grid=(b, seq_len // tq),
        in_specs=in_specs,
        out_specs=pl.BlockSpec((1, tq, GROUP_WIDTH), lambda bi, ti: (bi, ti, 0)),
        out_shape=jax.ShapeDtypeStruct((b, seq_len, GROUP_WIDTH), BF16),
        scratch_shapes=scratch,
        compiler_params=_params(2),
        name="attention",
    )(*args)


def _mixer_out(x, oa, sa, mb, wpa_ref, wout_ref):
    y_a = jnp.dot(oa, wpa_ref[...], preferred_element_type=F32)
    merged = sa.astype(F32) * y_a + mb.astype(F32)
    return x + jnp.dot(merged.astype(BF16), wout_ref[...], preferred_element_type=F32)


def _ffn_halo_body(x_ref, oa_ref, sa_ref, mb_ref, wpa_ref, wout_ref, nf_ref, wup_ref, out_ref):
    x1 = _mixer_out(x_ref[...], oa_ref[...], sa_ref[...], mb_ref[...], wpa_ref, wout_ref)
    h = _rmsnorm(x1, nf_ref[...]).astype(BF16)
    out_ref[...] = jnp.dot(h, wup_ref[...], preferred_element_type=F32)


def _ffn_halo(rows_of, wpa, wout, nf, wup, tm):
    b, s, _ = rows_of[0].shape
    nt = s // tm

    def edge_rows(a):
        at = a.reshape(b, nt, tm, a.shape[2])
        zero = jnp.zeros((b, 1, a.shape[2]), a.dtype)
        before = jnp.concatenate([zero, at[:, :-1, tm - 1, :]], axis=1)
        after = jnp.concatenate([at[:, 1:, 0, :], zero], axis=1)
        return jnp.stack([before, after], axis=2).reshape(b * nt * 2, a.shape[2])

    out = pl.pallas_call(
        _ffn_halo_body,
        out_shape=jax.ShapeDtypeStruct((b * nt * 2, wup.shape[1]), F32),
        compiler_params=pltpu.CompilerParams(vmem_limit_bytes=VMEM_LIMIT_BYTES),
        name="ffn_halo",
    )(*[edge_rows(a) for a in rows_of], wpa, wout, nf, wup)
    return out.reshape(b, nt, 2, wup.shape[1])


def _conv_ffn_body(x_ref, oa_ref, sa_ref, mb_ref, halo_ref, wpa_ref, wout_ref, nf_ref, wup_ref, cw_ref,
                   cb_ref, wdown_ref, nfin_ref, out_ref, x1_ref, perm_ref, act_ref, *, final_norm):
    tm = x_ref.shape[1]
    n_slabs = perm_ref.shape[0]
    run = tm // SUBLANES
    x1_ref[...] = _mixer_out(x_ref[0], oa_ref[0], sa_ref[0], mb_ref[0], wpa_ref, wout_ref)
    for s in range(SUBLANES):
        for c in range(n_slabs):
            perm_ref[c, pl.ds(s, run, stride=SUBLANES), :] = x1_ref[s * run:(s + 1) * run,
                                                                    c * LANES:(c + 1) * LANES]
    xp = jnp.concatenate([perm_ref[c] for c in range(n_slabs)], axis=1)
    h = _rmsnorm(xp, nf_ref[...]).astype(BF16)
    sublane = lax.broadcasted_iota(jnp.int32, (SUBLANES, 1), 0)
    half = FF_CHUNK // 2
    for c in range(wup_ref.shape[1] // FF_CHUNK):
        cols = slice(c * FF_CHUNK, (c + 1) * FF_CHUNK)
        a = jnp.dot(h, wup_ref[:, cols], preferred_element_type=F32)
        halo = halo_ref[0, 0, :, cols]
        wrap_prev = jnp.where(sublane == 0, halo[0:1], pltpu.roll(a[-SUBLANES:], 1, 0))
        wrap_next = jnp.where(sublane == SUBLANES - 1, halo[1:2], pltpu.roll(a[:SUBLANES], SUBLANES - 1, 0))
        prev = jnp.concatenate([wrap_prev, a[:-SUBLANES]], axis=0)
        nxt = jnp.concatenate([a[SUBLANES:], wrap_next], axis=0)
        cw = cw_ref[:, cols]
        conv = prev * cw[0:1] + a * cw[1:2] + nxt * cw[2:3] + cb_ref[:, cols]
        act_ref[:, c * half:(c + 1) * half] = (_gelu_tanh_exp(conv[:, :half]) * conv[:, half:]).astype(BF16)
    y = jnp.dot(act_ref[...], wdown_ref[...], preferred_element_type=F32)
    for c in range(n_slabs):
        perm_ref[c] = y[:, c * LANES:(c + 1) * LANES]
    for s in range(SUBLANES):
        rows = slice(s * run, (s + 1) * run)
        out = x1_ref[rows, :] + jnp.concatenate(
            [perm_ref[c, pl.ds(s, run, stride=SUBLANES), :] for c in range(n_slabs)], axis=1)
        if final_norm:
            out = _rmsnorm(out, nfin_ref[...])
        out_ref[0, rows, :] = out


def _conv_ffn(x, oa, sa, mb, wpa, wout, nf, wup, cw, cb, wdown, nfin, final_norm):
    b, s, d_model = x.shape
    tm = min(ROW_TILE, s)
    d_ff = wdown.shape[0]
    assert s % tm == 0 and (2 * d_ff) % FF_CHUNK == 0 and tm % (SUBLANES * SUBLANES) == 0
    halo = _ffn_halo((x, oa, sa, mb), wpa, wout, nf, wup, tm)

    def rows(width):
        return pl.BlockSpec((1, tm, width), lambda bi, ti: (bi, ti, 0))

    consts = (wpa, wout, nf, wup, cw, cb, wdown, nfin)
    return pl.pallas_call(
        functools.partial(_conv_ffn_body, final_norm=final_norm),
        grid=(b, s // tm),
        in_specs=[rows(d_model), rows(oa.shape[2]), rows(d_model), rows(d_model),
                  pl.BlockSpec((1, 1, 2, 2 * d_ff), lambda bi, ti: (bi, ti, 0, 0))]
                 + [_const_spec(a.shape) for a in consts],
        out_specs=rows(d_model),
        out_shape=jax.ShapeDtypeStruct(x.shape, F32),
        scratch_shapes=[pltpu.VMEM((tm, d_model), F32), pltpu.VMEM((d_model // LANES, tm, LANES), F32),
                        pltpu.VMEM((tm, d_ff), BF16)],
        compiler_params=_params(2),
        name="conv_ffn",
    )(x, oa, sa, mb, halo, *consts)


def _rel_bucket(rel):
    nb = NUM_BUCKETS // 2
    max_exact = nb // 2
    ret = np.where(rel > 0, nb, 0)
    n = np.abs(rel)
    nf = np.maximum(n, 1).astype(np.float32)
    large = max_exact + (np.log(nf / max_exact) / math.log(MAX_DISTANCE / max_exact)
                         * (nb - max_exact)).astype(np.int32)
    large = np.minimum(large, nb - 1)
    return (ret + np.where(n < max_exact, n, large)).astype(np.int32)


def _score_bias(rel_bias):
    period = Q_BLOCK + KEY_BLOCK
    rel = np.arange(period) - (Q_BLOCK - 1) - HALF_WINDOW
    in_band = jnp.asarray(np.abs(rel) <= HALF_WINDOW)
    col = np.arange(KEY_BLOCK)
    col_masks = [np.zeros(KEY_BLOCK, bool), col < HALF_WINDOW, col >= Q_BLOCK + HALF_WINDOW]
    col_masks.append(col_masks[1] | col_masks[2])
    out = []
    for g, (_, dil) in enumerate(DILATED_GROUPS):
        tab = rel_bias[:, g * HEADS_PER_GROUP:(g + 1) * HEADS_PER_GROUP].astype(F32)
        by_offset = jnp.where(in_band[:, None], tab[jnp.asarray(_rel_bucket(rel * dil))] * LOG2E, MASK_VALUE)
        by_offset = jnp.roll(by_offset.T, -(Q_BLOCK - 1), axis=1)
        bias = jnp.tile(by_offset, (1, Q_BLOCK))[:, :Q_BLOCK * (period - 1)]
        bias = bias.reshape(HEADS_PER_GROUP, Q_BLOCK, period - 1)[:, :, :KEY_BLOCK]
        bias = bias.reshape(HEADS_PER_GROUP // 2, 2 * Q_BLOCK, KEY_BLOCK)
        out.append(jnp.stack([jnp.where(jnp.asarray(cm), MASK_VALUE, bias) for cm in col_masks]))
    return out


def _layer_params(l, norm_mix, w_in, v_gain, w_s, b_s, w_proj_a, w_proj_b, w_out, norm_ffn, w_up,
                  conv_w, conv_b, w_down):
    attn_w = N_GROUPS * GROUP_WIDTH
    width = v_gain.shape[1]
    d_ff = w_down.shape[1]
    w = w_in[l].astype(BF16)
    wq = [jnp.concatenate([w[:, part * attn_w + g * GROUP_WIDTH: part * attn_w + (g + 1) * GROUP_WIDTH]
                           for part in range(3)], axis=1) for g in range(N_GROUPS)]
    wuv = w[:, 3 * attn_w: 3 * attn_w + 2 * width]
    wg = w[:, 3 * attn_w + 2 * width:]
    zb = jnp.repeat(b_s[l].T.astype(F32), width // SG_GROUPS, axis=1)

    def pair_columns(a):
        half = FF_CHUNK // 2
        lead = a.shape[:-1]
        return a.reshape(lead + (2, d_ff // half, half)).swapaxes(-3, -2).reshape(lead + (2 * d_ff,))

    return dict(
        nm=norm_mix[l][None, :], wq=wq, wuv=wuv, wg=wg, vgain=v_gain[l][None, :],
        ws=w_s[l].astype(BF16), zb=zb, wpb=w_proj_b[l].astype(BF16),
        wpa=w_proj_a[l].astype(BF16), wout=w_out[l].astype(BF16),
        nf=norm_ffn[l][None, :], wup=pair_columns(w_up[l].astype(BF16)), cw=pair_columns(conv_w[l]),
        cb=pair_columns(conv_b[l][None, :]), wdown=w_down[l].astype(BF16),
    )


def _trunk(x, layers, biases, nfin):
    seq_len = x.shape[1]
    for i, p in enumerate(layers):
        *qkvs, sa, mb = _mixer_in(x, p["nm"], p["wq"], p["wuv"], p["wg"], p["vgain"], p["ws"], p["zb"],
                                  p["wpb"])
        oa = _attention(qkvs, biases, seq_len)
        x = _conv_ffn(x, oa, sa, mb, p["wpa"], p["wout"], p["nf"], p["wup"], p["cw"], p["cb"], p["wdown"],
                      nfin, final_norm=(i == len(layers) - 1))
    return x


def kernel(x_prompt, x_sample, rel_bias, norm_mix, w_in, v_gain, w_s, b_s, w_proj_a, w_proj_b, w_out,
           norm_ffn, w_up, conv_w, conv_b, w_down, norm_final):
    depth = w_in.shape[0]
    layers = [_layer_params(l, norm_mix, w_in, v_gain, w_s, b_s, w_proj_a, w_proj_b, w_out, norm_ffn,
                            w_up, conv_w, conv_b, w_down) for l in range(depth)]
    biases = _score_bias(rel_bias)
    nfin = norm_final[None, :]
    return (_trunk(x_prompt, layers, biases, nfin), _trunk(x_sample, layers, biases, nfin))
```

```python
import functools
import math

import jax
import jax.numpy as jnp
import numpy as np
from jax import lax
from jax.experimental import pallas as pl
from jax.experimental.pallas import tpu as pltpu

F32 = jnp.float32
BF16 = jnp.bfloat16

HEAD_DIM = 64
HEADS_PER_GROUP = 4
GROUP_WIDTH = HEADS_PER_GROUP * HEAD_DIM
DILATED_GROUPS = ((128, 1), (512, 4), (2048, 16))
N_GROUPS = len(DILATED_GROUPS)
HALF_WINDOW = 64
NUM_BUCKETS = 32
MAX_DISTANCE = 1024
SG_CHUNK = 128
SG_GROUPS = 8
CONV_WIDTH = 3
EPS = 1e-6
MASK_VALUE = -1e30
LOG2E = math.log2(math.e)

LANES = 128
SUBLANES = 8
BF16_ROWS = 16
VMEM_LIMIT_BYTES = 56 * 1024 * 1024

ROW_TILE = 512
ATTN_TILE = 2048
Q_BLOCK = 128
KEY_BLOCK = Q_BLOCK + 2 * HALF_WINDOW
ATTN_UNROLL = 8
FF_CHUNK = 2 * LANES
MERGE_ROWS = 256


def _rmsnorm(x, gain):
    y = x * lax.rsqrt(jnp.mean(x * x, axis=-1, keepdims=True) + EPS)
    return y * gain


def _gelu_tanh(x):
    c = math.sqrt(2.0 / math.pi)
    return 0.5 * x * (1.0 + jnp.tanh(c * (x + 0.044715 * (x * x * x))))


def _gelu_tanh_exp(x):
    k0 = -2.0 * math.sqrt(2.0 / math.pi) * LOG2E
    return x / (1.0 + jnp.exp2(x * (k0 + (0.044715 * k0) * (x * x))))


def _sigmoid(x):
    return 0.5 * jnp.tanh(0.5 * x) + 0.5


def _const_spec(shape):
    zeros = (0,) * len(shape)
    return pl.BlockSpec(shape, lambda *_: zeros, pipeline_mode=pl.Buffered(1))


def _params(n_grid_axes):
    return pltpu.CompilerParams(
        dimension_semantics=("arbitrary",) * n_grid_axes,
        vmem_limit_bytes=VMEM_LIMIT_BYTES,
    )


def _mixer_in_body(x_ref, nm_ref, wq0_ref, wq1_ref, wq2_ref, wuv_ref, wg_ref, vgain_ref, ws_ref,
                   zb_ref, wpb_ref, qkv0_ref, qkv1_ref, qkv2_ref, sa_ref, mb_ref, perm_ref):
    tm = x_ref.shape[1]
    width = vgain_ref.shape[1]
    h = _rmsnorm(x_ref[0], nm_ref[...]).astype(BF16)

    uv = jnp.dot(h, wuv_ref[...], preferred_element_type=F32)
    gates = jnp.dot(h, wg_ref[...], preferred_element_type=F32)

    col = lax.broadcasted_iota(jnp.int32, (1, 3 * GROUP_WIDTH), 1)
    qscale = jnp.where(col < GROUP_WIDTH, LOG2E * HEAD_DIM ** -0.5, 1.0).astype(F32)
    for (_, dil), w_ref, out_ref in zip(DILATED_GROUPS, (wq0_ref, wq1_ref, wq2_ref),
                                        (qkv0_ref, qkv1_ref, qkv2_ref)):
        r = jnp.dot(h, w_ref[...], preferred_element_type=F32) * qscale
        if dil == 1:
            out_ref[0, 0] = r.astype(BF16)
        else:
            for c in range(perm_ref.shape[0]):
                perm_ref[c] = r[:, c * LANES:(c + 1) * LANES]
            for res in range(dil):
                for c in range(perm_ref.shape[0]):
                    out_ref[0, res, :, c * LANES:(c + 1) * LANES] = (
                        perm_ref[c, pl.ds(res, tm // dil, stride=dil), :].astype(BF16))

    d_model = gates.shape[1] // 2
    sa_ref[0] = _sigmoid(gates[:, :d_model]).astype(BF16)

    uv = _gelu_tanh(uv)
    u = uv[:, :width]
    vn = _rmsnorm(uv[:, width:], vgain_ref[...]).astype(BF16)
    gch = width // SG_GROUPS
    n_chunks = tm // SG_CHUNK
    mixed = []
    for g in range(SG_GROUPS):
        blocks = jnp.concatenate([vn[c * SG_CHUNK:(c + 1) * SG_CHUNK, g * gch:(g + 1) * gch]
                                  for c in range(n_chunks)], axis=1)
        mixed.append(jnp.dot(ws_ref[g], blocks, preferred_element_type=F32))
    z = jnp.concatenate(
        [jnp.concatenate([mixed[g][:, c * gch:(c + 1) * gch] for g in range(SG_GROUPS)], axis=1) + zb_ref[...]
         for c in range(n_chunks)], axis=0)
    y_b = jnp.dot((u * z).astype(BF16), wpb_ref[...], preferred_element_type=F32)
    mb_ref[0] = (_sigmoid(gates[:, d_model:]) * y_b).astype(BF16)


def _mixer_in(x, nm, wq, wuv, wg, vgain, ws, zb, wpb):
    b, s, d_model = x.shape
    tm = min(ROW_TILE, s)
    assert s % tm == 0 and tm % SG_CHUNK == 0
    max_dil = max(d for _, d in DILATED_GROUPS)
    assert tm % (max_dil * BF16_ROWS) == 0
    qkv_w = 3 * GROUP_WIDTH
    out_shape = [jax.ShapeDtypeStruct((b, dil, s // dil, qkv_w), BF16) for _, dil in DILATED_GROUPS]
    out_shape += [jax.ShapeDtypeStruct((b, s, d_model), BF16)] * 2
    out_specs = [pl.BlockSpec((1, dil, tm // dil, qkv_w), lambda bi, ti: (bi, 0, ti, 0))
                 for _, dil in DILATED_GROUPS]
    out_specs += [pl.BlockSpec((1, tm, d_model), lambda bi, ti: (bi, ti, 0))] * 2
    in_specs = [pl.BlockSpec((1, tm, d_model), lambda bi, ti: (bi, ti, 0)), _const_spec(nm.shape)]
    in_specs += [_const_spec(w.shape) for w in wq]
    in_specs += [_const_spec(a.shape) for a in (wuv, wg, vgain, ws, zb, wpb)]
    return pl.pallas_call(
        _mixer_in_body,
        grid=(b, s // tm),
        in_specs=in_specs,
        out_specs=out_specs,
        out_shape=out_shape,
        scratch_shapes=[pltpu.VMEM((qkv_w // LANES, tm, LANES), F32)],
        compiler_params=_params(2),
        name="mixer_in",
    )(x, nm, *wq, wuv, wg, vgain, ws, zb, wpb)


def _attend_pair(q_pair, k_pair, v_pair, bias, low_half):
    zero = jnp.zeros_like(q_pair)
    q_stack = jnp.concatenate([jnp.where(low_half, q_pair, zero), jnp.where(low_half, zero, q_pair)],
                              axis=0)
    s = lax.dot_general(q_stack, k_pair, (((1,), (1,)), ((), ())), preferred_element_type=F32)
    s = s + bias
    m = jnp.max(s, axis=-1, keepdims=True)
    p = jnp.exp2(s - m)
    denom = jnp.sum(p, axis=-1, keepdims=True)
    pv = jnp.dot(p.astype(BF16), v_pair, preferred_element_type=F32)
    denom = jnp.where(low_half, denom[:Q_BLOCK], denom[Q_BLOCK:])
    out = jnp.where(low_half, pv[:Q_BLOCK], pv[Q_BLOCK:]) * (1.0 / denom)
    return out, jnp.where(low_half, m[:Q_BLOCK], m[Q_BLOCK:]) + jnp.log2(denom)


def _attention_body(*refs):
    n_in = 8 * N_GROUPS
    in_refs, out_ref, scratch = refs[:n_in], refs[n_in], refs[n_in + 1:]
    ti = pl.program_id(1)
    n_tiles = pl.num_programs(1)
    tq = out_ref.shape[1]
    lane = lax.broadcasted_iota(jnp.int32, (1, LANES), 1)
    low_half = lane < HEAD_DIM

    for g, (_, dil) in enumerate(DILATED_GROUPS):
        (q_ref, k_ref, kp_ref, kn_ref, v_ref, vp_ref, vn_ref, bias_ref) = in_refs[8 * g:8 * g + 8]
        kx_ref, vx_ref, o_ref, lse_ref = scratch[4 * g:4 * g + 4]
        tl = tq // dil
        n_blocks = tl // Q_BLOCK

        kx_ref[:, :HALF_WINDOW, :] = kp_ref[0]
        kx_ref[:, HALF_WINDOW:HALF_WINDOW + tl, :] = k_ref[0]
        kx_ref[:, HALF_WINDOW + tl:, :] = kn_ref[0]
        vx_ref[:, :HALF_WINDOW, :] = vp_ref[0]
        vx_ref[:, HALF_WINDOW:HALF_WINDOW + tl, :] = v_ref[0]
        vx_ref[:, HALF_WINDOW + tl:, :] = vn_ref[0]

        def blocks(step, carry, *, dil=dil, n_blocks=n_blocks, q_ref=q_ref, kx_ref=kx_ref,
                   vx_ref=vx_ref, bias_ref=bias_ref, o_ref=o_ref, lse_ref=lse_ref):
            for u in range(ATTN_UNROLL):
                it = step * ATTN_UNROLL + u
                res = it // n_blocks
                jb = it % n_blocks
                row0 = pl.multiple_of(jb * Q_BLOCK, Q_BLOCK)
                q_blk = q_ref[0, res, pl.ds(row0, Q_BLOCK), :]
                k_win = kx_ref[res, pl.ds(row0, KEY_BLOCK), :]
                v_win = vx_ref[res, pl.ds(row0, KEY_BLOCK), :]
                at_start = jnp.logical_and(ti == 0, jb == 0)
                at_end = jnp.logical_and(ti == n_tiles - 1, jb == n_blocks - 1)
                variant = at_start.astype(jnp.int32) + 2 * at_end.astype(jnp.int32)
                outs, lses = [], []
                for hp in range(HEADS_PER_GROUP // 2):
                    cols = slice(hp * LANES, (hp + 1) * LANES)
                    o_pair, lse_pair = _attend_pair(q_blk[:, cols], k_win[:, cols], v_win[:, cols],
                                                    bias_ref[variant, hp], low_half)
                    outs.append(o_pair)
                    lses.append(lse_pair)
                if dil == 1:
                    o_ref[pl.ds(row0, Q_BLOCK), :] = jnp.concatenate(outs, axis=1)
                    lse_ref[pl.ds(row0, Q_BLOCK), :] = jnp.concatenate(lses, axis=1)
                else:
                    start = row0 * dil + res
                    for c in range(GROUP_WIDTH // LANES):
                        o_ref[c, pl.ds(start, Q_BLOCK, stride=dil), :] = outs[c]
                        lse_ref[c, pl.ds(start, Q_BLOCK, stride=dil), :] = lses[c]
            return carry

        lax.fori_loop(0, dil * n_blocks // ATTN_UNROLL, blocks, 0)

    o0_ref, l0_ref = scratch[2], scratch[3]
    o1_ref, l1_ref = scratch[6], scratch[7]
    o2_ref, l2_ref = scratch[10], scratch[11]

    def merge(it, carry):
        row0 = pl.multiple_of(it * MERGE_ROWS, MERGE_ROWS)
        rows = pl.ds(row0, MERGE_ROWS)
        for c in range(GROUP_WIDTH // LANES):
            cols = slice(c * LANES, (c + 1) * LANES)
            l0, l1, l2 = l0_ref[rows, cols], l1_ref[c, rows, :], l2_ref[c, rows, :]
            mx = jnp.maximum(jnp.maximum(l0, l1), l2)
            w0, w1, w2 = jnp.exp2(l0 - mx), jnp.exp2(l1 - mx), jnp.exp2(l2 - mx)
            num = w0 * o0_ref[rows, cols] + w1 * o1_ref[c, rows, :] + w2 * o2_ref[c, rows, :]
            out_ref[0, rows, cols] = (num / (w0 + w1 + w2)).astype(BF16)
        return carry

    lax.fori_loop(0, tq // MERGE_ROWS, merge, 0)


def _attention(qkvs, biases, seq_len):
    b = qkvs[0].shape[0]
    tq = min(ATTN_TILE, seq_len)
    assert seq_len % tq == 0
    in_specs, args, scratch = [], [], []
    for (_, dil), qkv, bias in zip(DILATED_GROUPS, qkvs, biases):
        tl = tq // dil
        assert tl % Q_BLOCK == 0 and (tq // Q_BLOCK) % ATTN_UNROLL == 0
        halo_per_tile = tl // HALF_WINDOW
        n_halo = seq_len // dil // HALF_WINDOW

        def main(col):
            return pl.BlockSpec((1, dil, tl, GROUP_WIDTH), lambda bi, ti, col=col: (bi, 0, ti, col))

        def before(col, hpt=halo_per_tile):
            return pl.BlockSpec((1, dil, HALF_WINDOW, GROUP_WIDTH),
                                lambda bi, ti, col=col: (bi, 0, jnp.maximum(ti * hpt - 1, 0), col))

        def after(col, hpt=halo_per_tile, n_halo=n_halo):
            return pl.BlockSpec((1, dil, HALF_WINDOW, GROUP_WIDTH),
                                lambda bi, ti, col=col: (bi, 0, jnp.minimum((ti + 1) * hpt, n_halo - 1), col))

        in_specs += [main(0), main(1), before(1), after(1), main(2), before(2), after(2),
                     _const_spec(bias.shape)]
        args += [qkv] * 7 + [bias]
        ext = (dil, tl + 2 * HALF_WINDOW, GROUP_WIDTH)
        if dil == 1:
            out_scr = (tq, GROUP_WIDTH)
        else:
            out_scr = (GROUP_WIDTH // LANES, tq, LANES)
        scratch += [pltpu.VMEM(ext, BF16), pltpu.VMEM(ext, BF16),
                    pltpu.VMEM(out_scr, F32), pltpu.VMEM(out_scr, F32)]
    return pl.pallas_call(
        _attention_body,
        grid=(b, seq_len // tq),
        in_specs=in_specs,
        out_specs=pl.BlockSpec((1, tq, GROUP_WIDTH), lambda bi, ti: (bi, ti, 0)),
        out_shape=jax.ShapeDtypeStruct((b, seq_len, GROUP_WIDTH), BF16),
        scratch_shapes=scratch,
        compiler_params=_params(2),
        name="attention",
    )(*args)


def _mixer_out(x, oa, sa, mb, wpa_ref, wout_ref):
    y_a = jnp.dot(oa, wpa_ref[...], preferred_element_type=F32)
    merged = sa.astype(F32) * y_a + mb.astype(F32)
    return x + jnp.dot(merged.astype(BF16), wout_ref[...], preferred_element_type=F32)


def _ffn_halo_body(x_ref, oa_ref, sa_ref, mb_ref, wpa_ref, wout_ref, nf_ref, wup_ref, out_ref):
    x1 = _mixer_out(x_ref[...], oa_ref[...], sa_ref[...], mb_ref[...], wpa_ref, wout_ref)
    h = _rmsnorm(x1, nf_ref[...]).astype(BF16)
    out_ref[...] = jnp.dot(h, wup_ref[...], preferred_element_type=F32)


def _ffn_halo(rows_of, wpa, wout, nf, wup, tm):
    b, s, _ = rows_of[0].shape
    nt = s // tm

    def edge_rows(a):
        at = a.reshape(b, nt, tm, a.shape[2])
        zero = jnp.zeros((b, 1, a.shape[2]), a.dtype)
        before = jnp.concatenate([zero, at[:, :-1, tm - 1, :]], axis=1)
        after = jnp.concatenate([at[:, 1:, 0, :], zero], axis=1)
        return jnp.stack([before, after], axis=2).reshape(b * nt * 2, a.shape[2])

    return pl.pallas_call(
        _ffn_halo_body,
        out_shape=jax.ShapeDtypeStruct((b * nt * 2, wup.shape[1]), F32),
        compiler_params=pltpu.CompilerParams(vmem_limit_bytes=VMEM_LIMIT_BYTES),
        name="ffn_halo",
    )(*[edge_rows(a) for a in rows_of], wpa, wout, nf, wup)


def _conv_ffn_body(x_ref, oa_ref, sa_ref, mb_ref, halo_ref, wpa_ref, wout_ref, nf_ref, wup_ref, cw_ref,
                   cb_ref, wdown_ref, nfin_ref, out_ref, x1_ref, perm_ref, act_ref, *, final_norm):
    tm = x_ref.shape[1]
    n_slabs = perm_ref.shape[0]
    run = tm // SUBLANES
    tile = pl.program_id(0) * pl.num_programs(1) + pl.program_id(1)
    halo_row = (tile % (SUBLANES // 2)) * 2
    x1_ref[...] = _mixer_out(x_ref[0], oa_ref[0], sa_ref[0], mb_ref[0], wpa_ref, wout_ref)
    for s in range(SUBLANES):
        for c in range(n_slabs):
            perm_ref[c, pl.ds(s, run, stride=SUBLANES), :] = x1_ref[s * run:(s + 1) * run,
                                                                    c * LANES:(c + 1) * LANES]
    xp = jnp.concatenate([perm_ref[c] for c in range(n_slabs)], axis=1)
    h = _rmsnorm(xp, nf_ref[...]).astype(BF16)
    sublane = lax.broadcasted_iota(jnp.int32, (SUBLANES, 1), 0)
    half = FF_CHUNK // 2
    for c in range(wup_ref.shape[1] // FF_CHUNK):
        cols = slice(c * FF_CHUNK, (c + 1) * FF_CHUNK)
        a = jnp.dot(h, wup_ref[:, cols], preferred_element_type=F32)
        wrap_prev = jnp.where(sublane == 0, halo_ref[pl.ds(halo_row, 1), cols],
                              pltpu.roll(a[-SUBLANES:], 1, 0))
        wrap_next = jnp.where(sublane == SUBLANES - 1, halo_ref[pl.ds(halo_row + 1, 1), cols],
                              pltpu.roll(a[:SUBLANES], SUBLANES - 1, 0))
        prev = jnp.concatenate([wrap_prev, a[:-SUBLANES]], axis=0)
        nxt = jnp.concatenate([a[SUBLANES:], wrap_next], axis=0)
        cw = cw_ref[:, cols]
        conv = prev * cw[0:1] + a * cw[1:2] + nxt * cw[2:3] + cb_ref[:, cols]
        act_ref[:, c * half:(c + 1) * half] = (_gelu_tanh_exp(conv[:, :half]) * conv[:, half:]).astype(BF16)
    y = jnp.dot(act_ref[...], wdown_ref[...], preferred_element_type=F32)
    for c in range(n_slabs):
        perm_ref[c] = y[:, c * LANES:(c + 1) * LANES]
    for s in range(SUBLANES):
        rows = slice(s * run, (s + 1) * run)
        out = x1_ref[rows, :] + jnp.concatenate(
            [perm_ref[c, pl.ds(s, run, stride=SUBLANES), :] for c in range(n_slabs)], axis=1)
        if final_norm:
            out = _rmsnorm(out, nfin_ref[...])
        out_ref[0, rows, :] = out


def _conv_ffn(x, oa, sa, mb, wpa, wout, nf, wup, cw, cb, wdown, nfin, final_norm):
    b, s, d_model = x.shape
    tm = min(ROW_TILE, s)
    d_ff = wdown.shape[0]
    assert s % tm == 0 and (2 * d_ff) % FF_CHUNK == 0 and tm % (SUBLANES * SUBLANES) == 0
    nt = s // tm
    assert (b * nt * 2) % SUBLANES == 0
    halo = _ffn_halo((x, oa, sa, mb), wpa, wout, nf, wup, tm)

    def rows(width):
        return pl.BlockSpec((1, tm, width), lambda bi, ti: (bi, ti, 0))

    consts = (wpa, wout, nf, wup, cw, cb, wdown, nfin)
    return pl.pallas_call(
        functools.partial(_conv_ffn_body, final_norm=final_norm),
        grid=(b, s // tm),
        in_specs=[rows(d_model), rows(oa.shape[2]), rows(d_model), rows(d_model),
                  pl.BlockSpec((SUBLANES, 2 * d_ff), lambda bi, ti: ((bi * nt + ti) // (SUBLANES // 2), 0))]
                 + [_const_spec(a.shape) for a in consts],
        out_specs=rows(d_model),
        out_shape=jax.ShapeDtypeStruct(x.shape, F32),
        scratch_shapes=[pltpu.VMEM((tm, d_model), F32), pltpu.VMEM((d_model // LANES, tm, LANES), F32),
                        pltpu.VMEM((tm, d_ff), BF16)],
        compiler_params=_params(2),
        name="conv_ffn",
    )(x, oa, sa, mb, halo, *consts)


def _rel_bucket(rel):
    nb = NUM_BUCKETS // 2
    max_exact = nb // 2
    ret = np.where(rel > 0, nb, 0)
    n = np.abs(rel)
    nf = np.maximum(n, 1).astype(np.float32)
    large = max_exact + (np.log(nf / max_exact) / math.log(MAX_DISTANCE / max_exact)
                         * (nb - max_exact)).astype(np.int32)
    large = np.minimum(large, nb - 1)
    return (ret + np.where(n < max_exact, n, large)).astype(np.int32)


def _score_bias(rel_bias):
    period = Q_BLOCK + KEY_BLOCK
    rel = np.arange(period) - (Q_BLOCK - 1) - HALF_WINDOW
    in_band = jnp.asarray(np.abs(rel) <= HALF_WINDOW)
    col = np.arange(KEY_BLOCK)
    col_masks = [np.zeros(KEY_BLOCK, bool), col < HALF_WINDOW, col >= Q_BLOCK + HALF_WINDOW]
    col_masks.append(col_masks[1] | col_masks[2])
    out = []
    for g, (_, dil) in enumerate(DILATED_GROUPS):
        tab = rel_bias[:, g * HEADS_PER_GROUP:(g + 1) * HEADS_PER_GROUP].astype(F32)
        by_offset = jnp.where(in_band[:, None], tab[jnp.asarray(_rel_bucket(rel * dil))] * LOG2E, MASK_VALUE)
        by_offset = jnp.roll(by_offset.T, -(Q_BLOCK - 1), axis=1)
        bias = jnp.tile(by_offset, (1, Q_BLOCK))[:, :Q_BLOCK * (period - 1)]
        bias = bias.reshape(HEADS_PER_GROUP, Q_BLOCK, period - 1)[:, :, :KEY_BLOCK]
        bias = bias.reshape(HEADS_PER_GROUP // 2, 2 * Q_BLOCK, KEY_BLOCK)
        out.append(jnp.stack([jnp.where(jnp.asarray(cm), MASK_VALUE, bias) for cm in col_masks]))
    return out


def _layer_params(l, norm_mix, w_in, v_gain, w_s, b_s, w_proj_a, w_proj_b, w_out, norm_ffn, w_up,
                  conv_w, conv_b, w_down):
    attn_w = N_GROUPS * GROUP_WIDTH
    width = v_gain.shape[1]
    d_ff = w_down.shape[1]
    w = w_in[l].astype(BF16)
    wq = [jnp.concatenate([w[:, part * attn_w + g * GROUP_WIDTH: part * attn_w + (g + 1) * GROUP_WIDTH]
                           for part in range(3)], axis=1) for g in range(N_GROUPS)]
    wuv = w[:, 3 * attn_w: 3 * attn_w + 2 * width]
    wg = w[:, 3 * attn_w + 2 * width:]
    zb = jnp.repeat(b_s[l].T.astype(F32), width // SG_GROUPS, axis=1)

    def pair_columns(a):
        half = FF_CHUNK // 2
        lead = a.shape[:-1]
        return a.reshape(lead + (2, d_ff // half, half)).swapaxes(-3, -2).reshape(lead + (2 * d_ff,))

    return dict(
        nm=norm_mix[l][None, :], wq=wq, wuv=wuv, wg=wg, vgain=v_gain[l][None, :],
        ws=w_s[l].astype(BF16), zb=zb, wpb=w_proj_b[l].astype(BF16),
        wpa=w_proj_a[l].astype(BF16), wout=w_out[l].astype(BF16),
        nf=norm_ffn[l][None, :], wup=pair_columns(w_up[l].astype(BF16)), cw=pair_columns(conv_w[l]),
        cb=pair_columns(conv_b[l][None, :]), wdown=w_down[l].astype(BF16),
    )


def _trunk(x, layers, biases, nfin):
    seq_len = x.shape[1]
    for i, p in enumerate(layers):
        *qkvs, sa, mb = _mixer_in(x, p["nm"], p["wq"], p["wuv"], p["wg"], p["vgain"], p["ws"], p["zb"],
                                  p["wpb"])
        oa = _attention(qkvs, biases, seq_len)
        x = _conv_ffn(x, oa, sa, mb, p["wpa"], p["wout"], p["nf"], p["wup"], p["cw"], p["cb"], p["wdown"],
                      nfin, final_norm=(i == len(layers) - 1))
    return x


def kernel(x_prompt, x_sample, rel_bias, norm_mix, w_in, v_gain, w_s, b_s, w_proj_a, w_proj_b, w_out,
           norm_ffn, w_up, conv_w, conv_b, w_down, norm_final):
    depth = w_in.shape[0]
    layers = [_layer_params(l, norm_mix, w_in, v_gain, w_s, b_s, w_proj_a, w_proj_b, w_out, norm_ffn,
                            w_up, conv_w, conv_b, w_down) for l in range(depth)]
    biases = _score_bias(rel_bias)
    nfin = norm_final[None, :]
    return (_trunk(x_prompt, layers, biases, nfin), _trunk(x_sample, layers, biases, nfin))
```

```python
import functools
import math

import jax
import jax.numpy as jnp
import numpy as np
from jax import lax
from jax.experimental import pallas as pl
from jax.experimental.pallas import tpu as pltpu

F32 = jnp.float32
BF16 = jnp.bfloat16

HEAD_DIM = 64
HEADS_PER_GROUP = 4
GROUP_WIDTH = HEADS_PER_GROUP * HEAD_DIM
DILATED_GROUPS = ((128, 1), (512, 4), (2048, 16))
N_GROUPS = len(DILATED_GROUPS)
HALF_WINDOW = 64
NUM_BUCKETS = 32
MAX_DISTANCE = 1024
SG_CHUNK = 128
SG_GROUPS = 8
CONV_WIDTH = 3
EPS = 1e-6
MASK_VALUE = -1e30
LOG2E = math.log2(math.e)

LANES = 128
SUBLANES = 8
BF16_ROWS = 16
VMEM_LIMIT_BYTES = 56 * 1024 * 1024

ROW_TILE = 512
ATTN_TILE = 2048
Q_BLOCK = 128
KEY_BLOCK = Q_BLOCK + 2 * HALF_WINDOW
ATTN_UNROLL = 8
FF_CHUNK = 2 * LANES
MERGE_ROWS = 256


def _rmsnorm(x, gain):
    y = x * lax.rsqrt(jnp.mean(x * x, axis=-1, keepdims=True) + EPS)
    return y * gain


def _gelu_tanh(x):
    c = math.sqrt(2.0 / math.pi)
    return 0.5 * x * (1.0 + jnp.tanh(c * (x + 0.044715 * (x * x * x))))


def _gelu_tanh_exp(x):
    k0 = -2.0 * math.sqrt(2.0 / math.pi) * LOG2E
    return x / (1.0 + jnp.exp2(x * (k0 + (0.044715 * k0) * (x * x))))


def _sigmoid(x):
    return 0.5 * jnp.tanh(0.5 * x) + 0.5


def _const_spec(shape):
    zeros = (0,) * len(shape)
    return pl.BlockSpec(shape, lambda *_: zeros, pipeline_mode=pl.Buffered(1))


def _params(n_grid_axes):
    return pltpu.CompilerParams(
        dimension_semantics=("arbitrary",) * n_grid_axes,
        vmem_limit_bytes=VMEM_LIMIT_BYTES,
    )


def _mixer_in_body(x_ref, nm_ref, wq0_ref, wq1_ref, wq2_ref, wuv_ref, wg_ref, vgain_ref, ws_ref,
                   zb_ref, wpb_ref, qkv0_ref, qkv1_ref, qkv2_ref, sa_ref, mb_ref, perm_ref):
    tm = x_ref.shape[1]
    width = vgain_ref.shape[1]
    h = _rmsnorm(x_ref[0], nm_ref[...]).astype(BF16)

    uv = jnp.dot(h, wuv_ref[...], preferred_element_type=F32)
    gates = jnp.dot(h, wg_ref[...], preferred_element_type=F32)

    col = lax.broadcasted_iota(jnp.int32, (1, 3 * GROUP_WIDTH), 1)
    qscale = jnp.where(col < GROUP_WIDTH, LOG2E * HEAD_DIM ** -0.5, 1.0).astype(F32)
    for (_, dil), w_ref, out_ref in zip(DILATED_GROUPS, (wq0_ref, wq1_ref, wq2_ref),
                                        (qkv0_ref, qkv1_ref, qkv2_ref)):
        r = jnp.dot(h, w_ref[...], preferred_element_type=F32) * qscale
        if dil == 1:
            out_ref[0, 0] = r.astype(BF16)
        else:
            for c in range(perm_ref.shape[0]):
                perm_ref[c] = r[:, c * LANES:(c + 1) * LANES]
            for res in range(dil):
                for c in range(perm_ref.shape[0]):
                    out_ref[0, res, :, c * LANES:(c + 1) * LANES] = (
                        perm_ref[c, pl.ds(res, tm // dil, stride=dil), :].astype(BF16))

    d_model = gates.shape[1] // 2
    sig = _sigmoid(gates.astype(BF16))
    sa_ref[0] = sig[:, :d_model]

    uv = _gelu_tanh(uv.astype(BF16))
    u = uv[:, :width]
    vn = _rmsnorm(uv[:, width:].astype(F32), vgain_ref[...]).astype(BF16)
    gch = width // SG_GROUPS
    n_chunks = tm // SG_CHUNK
    mixed = []
    for g in range(SG_GROUPS):
        blocks = jnp.concatenate([vn[c * SG_CHUNK:(c + 1) * SG_CHUNK, g * gch:(g + 1) * gch]
                                  for c in range(n_chunks)], axis=1)
        mixed.append(jnp.dot(ws_ref[g], blocks, preferred_element_type=F32))
    z = jnp.concatenate(
        [jnp.concatenate([mixed[g][:, c * gch:(c + 1) * gch] for g in range(SG_GROUPS)], axis=1) + zb_ref[...]
         for c in range(n_chunks)], axis=0)
    y_b = jnp.dot(u * z.astype(BF16), wpb_ref[...], preferred_element_type=F32)
    mb_ref[0] = sig[:, d_model:] * y_b.astype(BF16)


def _mixer_in(x, nm, wq, wuv, wg, vgain, ws, zb, wpb):
    b, s, d_model = x.shape
    tm = min(ROW_TILE, s)
    assert s % tm == 0 and tm % SG_CHUNK == 0
    max_dil = max(d for _, d in DILATED_GROUPS)
    assert tm % (max_dil * BF16_ROWS) == 0
    qkv_w = 3 * GROUP_WIDTH
    out_shape = [jax.ShapeDtypeStruct((b, dil, s // dil, qkv_w), BF16) for _, dil in DILATED_GROUPS]
    out_shape += [jax.ShapeDtypeStruct((b, s, d_model), BF16)] * 2
    out_specs = [pl.BlockSpec((1, dil, tm // dil, qkv_w), lambda bi, ti: (bi, 0, ti, 0))
                 for _, dil in DILATED_GROUPS]
    out_specs += [pl.BlockSpec((1, tm, d_model), lambda bi, ti: (bi, ti, 0))] * 2
    in_specs = [pl.BlockSpec((1, tm, d_model), lambda bi, ti: (bi, ti, 0)), _const_spec(nm.shape)]
    in_specs += [_const_spec(w.shape) for w in wq]
    in_specs += [_const_spec(a.shape) for a in (wuv, wg, vgain, ws, zb, wpb)]
    return pl.pallas_call(
        _mixer_in_body,
        grid=(b, s // tm),
        in_specs=in_specs,
        out_specs=out_specs,
        out_shape=out_shape,
        scratch_shapes=[pltpu.VMEM((qkv_w // LANES, tm, LANES), F32)],
        compiler_params=_params(2),
        name="mixer_in",
    )(x, nm, *wq, wuv, wg, vgain, ws, zb, wpb)


def _attend_pair(q_pair, k_pair, v_pair, bias, low_half):
    zero = jnp.zeros_like(q_pair)
    q_stack = jnp.concatenate([jnp.where(low_half, q_pair, zero), jnp.where(low_half, zero, q_pair)],
                              axis=0)
    s = lax.dot_general(q_stack, k_pair, (((1,), (1,)), ((), ())), preferred_element_type=F32)
    s = s + bias
    m = jnp.max(s, axis=-1, keepdims=True)
    p = jnp.exp2(s - m)
    denom = jnp.sum(p, axis=-1, keepdims=True)
    pv = jnp.dot(p.astype(BF16), v_pair, preferred_element_type=F32)
    denom = jnp.where(low_half, denom[:Q_BLOCK], denom[Q_BLOCK:])
    out = jnp.where(low_half, pv[:Q_BLOCK], pv[Q_BLOCK:]) * (1.0 / denom)
    return out, jnp.where(low_half, m[:Q_BLOCK], m[Q_BLOCK:]) + jnp.log2(denom)


def _attention_body(*refs):
    n_in = 8 * N_GROUPS
    in_refs, out_ref, scratch = refs[:n_in], refs[n_in], refs[n_in + 1:]
    ti = pl.program_id(1)
    n_tiles = pl.num_programs(1)
    tq = out_ref.shape[1]
    lane = lax.broadcasted_iota(jnp.int32, (1, LANES), 1)
    low_half = lane < HEAD_DIM

    for g, (_, dil) in enumerate(DILATED_GROUPS):
        (q_ref, k_ref, kp_ref, kn_ref, v_ref, vp_ref, vn_ref, bias_ref) = in_refs[8 * g:8 * g + 8]
        kx_ref, vx_ref, o_ref, lse_ref = scratch[4 * g:4 * g + 4]
        tl = tq // dil
        n_blocks = tl // Q_BLOCK

        kx_ref[:, :HALF_WINDOW, :] = kp_ref[0]
        kx_ref[:, HALF_WINDOW:HALF_WINDOW + tl, :] = k_ref[0]
        kx_ref[:, HALF_WINDOW + tl:, :] = kn_ref[0]
        vx_ref[:, :HALF_WINDOW, :] = vp_ref[0]
        vx_ref[:, HALF_WINDOW:HALF_WINDOW + tl, :] = v_ref[0]
        vx_ref[:, HALF_WINDOW + tl:, :] = vn_ref[0]

        def blocks(step, carry, *, dil=dil, n_blocks=n_blocks, q_ref=q_ref, kx_ref=kx_ref,
                   vx_ref=vx_ref, bias_ref=bias_ref, o_ref=o_ref, lse_ref=lse_ref):
            for u in range(ATTN_UNROLL):
                it = step * ATTN_UNROLL + u
                res = it // n_blocks
                jb = it % n_blocks
                row0 = pl.multiple_of(jb * Q_BLOCK, Q_BLOCK)
                q_blk = q_ref[0, res, pl.ds(row0, Q_BLOCK), :]
                k_win = kx_ref[res, pl.ds(row0, KEY_BLOCK), :]
                v_win = vx_ref[res, pl.ds(row0, KEY_BLOCK), :]
                at_start = jnp.logical_and(ti == 0, jb == 0)
                at_end = jnp.logical_and(ti == n_tiles - 1, jb == n_blocks - 1)
                variant = at_start.astype(jnp.int32) + 2 * at_end.astype(jnp.int32)
                outs, lses = [], []
                for hp in range(HEADS_PER_GROUP // 2):
                    cols = slice(hp * LANES, (hp + 1) * LANES)
                    o_pair, lse_pair = _attend_pair(q_blk[:, cols], k_win[:, cols], v_win[:, cols],
                                                    bias_ref[variant, hp], low_half)
                    outs.append(o_pair)
                    lses.append(lse_pair)
                if dil == 1:
                    o_ref[pl.ds(row0, Q_BLOCK), :] = jnp.concatenate(outs, axis=1)
                    lse_ref[pl.ds(row0, Q_BLOCK), :] = jnp.concatenate(lses, axis=1)
                else:
                    start = row0 * dil + res
                    for c in range(GROUP_WIDTH // LANES):
                        o_ref[c, pl.ds(start, Q_BLOCK, stride=dil), :] = outs[c]
                        lse_ref[c, pl.ds(start, Q_BLOCK, stride=dil), :] = lses[c]
            return carry

        lax.fori_loop(0, dil * n_blocks // ATTN_UNROLL, blocks, 0)

    o0_ref, l0_ref = scratch[2], scratch[3]
    o1_ref, l1_ref = scratch[6], scratch[7]
    o2_ref, l2_ref = scratch[10], scratch[11]

    def merge(it, carry):
        row0 = pl.multiple_of(it * MERGE_ROWS, MERGE_ROWS)
        rows = pl.ds(row0, MERGE_ROWS)
        for c in range(GROUP_WIDTH // LANES):
            cols = slice(c * LANES, (c + 1) * LANES)
            l0, l1, l2 = l0_ref[rows, cols], l1_ref[c, rows, :], l2_ref[c, rows, :]
            mx = jnp.maximum(jnp.maximum(l0, l1), l2)
            w0, w1, w2 = jnp.exp2(l0 - mx), jnp.exp2(l1 - mx), jnp.exp2(l2 - mx)
            num = w0 * o0_ref[rows, cols] + w1 * o1_ref[c, rows, :] + w2 * o2_ref[c, rows, :]
            out_ref[0, rows, cols] = (num / (w0 + w1 + w2)).astype(BF16)
        return carry

    lax.fori_loop(0, tq // MERGE_ROWS, merge, 0)


def _attention(qkvs, biases, seq_len):
    b = qkvs[0].shape[0]
    tq = min(ATTN_TILE, seq_len)
    assert seq_len % tq == 0
    in_specs, args, scratch = [], [], []
    for (_, dil), qkv, bias in zip(DILATED_GROUPS, qkvs, biases):
        tl = tq // dil
        assert tl % Q_BLOCK == 0 and (tq // Q_BLOCK) % ATTN_UNROLL == 0
        halo_per_tile = tl // HALF_WINDOW
        n_halo = seq_len // dil // HALF_WINDOW

        def main(col):
            return pl.BlockSpec((1, dil, tl, GROUP_WIDTH), lambda bi, ti, col=col: (bi, 0, ti, col))

        def before(col, hpt=halo_per_tile):
            return pl.BlockSpec((1, dil, HALF_WINDOW, GROUP_WIDTH),
                                lambda bi, ti, col=col: (bi, 0, jnp.maximum(ti * hpt - 1, 0), col))

        def after(col, hpt=halo_per_tile, n_halo=n_halo):
            return pl.BlockSpec((1, dil, HALF_WINDOW, GROUP_WIDTH),
                                lambda bi, ti, col=col: (bi, 0, jnp.minimum((ti + 1) * hpt, n_halo - 1), col))

        in_specs += [main(0), main(1), before(1), after(1), main(2), before(2), after(2),
                     _const_spec(bias.shape)]
        args += [qkv] * 7 + [bias]
        ext = (dil, tl + 2 * HALF_WINDOW, GROUP_WIDTH)
        if dil == 1:
            out_scr = (tq, GROUP_WIDTH)
        else:
            out_scr = (GROUP_WIDTH // LANES, tq, LANES)
        scratch += [pltpu.VMEM(ext, BF16), pltpu.VMEM(ext, BF16),
                    pltpu.VMEM(out_scr, F32), pltpu.VMEM(out_scr, F32)]
    return pl.pallas_call(
        _attention_body,
        grid=(b, seq_len // tq),
        in_specs=in_specs,
        out_specs=pl.BlockSpec((1, tq, GROUP_WIDTH), lambda bi, ti: (bi, ti, 0)),
        out_shape=jax.ShapeDtypeStruct((b, seq_len, GROUP_WIDTH), BF16),
        scratch_shapes=scratch,
        compiler_params=_params(2),
        name="attention",
    )(*args)


def _mixer_out(x, oa, sa, mb, wpa_ref, wout_ref):
    y_a = jnp.dot(oa, wpa_ref[...], preferred_element_type=F32)
    merged = sa.astype(F32) * y_a + mb.astype(F32)
    return x + jnp.dot(merged.astype(BF16), wout_ref[...], preferred_element_type=F32)


def _ffn_halo_body(x_ref, oa_ref, sa_ref, mb_ref, wpa_ref, wout_ref, nf_ref, wup_ref, out_ref):
    x1 = _mixer_out(x_ref[...], oa_ref[...], sa_ref[...], mb_ref[...], wpa_ref, wout_ref)
    h = _rmsnorm(x1, nf_ref[...]).astype(BF16)
    out_ref[...] = jnp.dot(h, wup_ref[...], preferred_element_type=F32)


def _ffn_halo(rows_of, wpa, wout, nf, wup, tm):
    b, s, _ = rows_of[0].shape
    nt = s // tm

    def edge_rows(a):
        zero = jnp.zeros((b, 1, a.shape[2]), a.dtype)
        before = jnp.concatenate([zero, a[:, tm - 1:s - 1:tm, :]], axis=1)
        after = jnp.concatenate([a[:, tm::tm, :], zero], axis=1)
        return jnp.stack([before, after], axis=2).reshape(b * nt * 2, a.shape[2])

    return pl.pallas_call(
        _ffn_halo_body,
        out_shape=jax.ShapeDtypeStruct((b * nt * 2, wup.shape[1]), F32),
        compiler_params=pltpu.CompilerParams(vmem_limit_bytes=VMEM_LIMIT_BYTES),
        name="ffn_halo",
    )(*[edge_rows(a) for a in rows_of], wpa, wout, nf, wup)


def _conv_ffn_body(x_ref, oa_ref, sa_ref, mb_ref, halo_ref, wpa_ref, wout_ref, nf_ref, wup_ref, cw_ref,
                   cb_ref, wdown_ref, nfin_ref, out_ref, x1_ref, perm_ref, act_ref, *, final_norm):
    tm = x_ref.shape[1]
    n_slabs = perm_ref.shape[0]
    run = tm // SUBLANES
    tile = pl.program_id(0) * pl.num_programs(1) + pl.program_id(1)
    halo_row = (tile % (SUBLANES // 2)) * 2
    x1_ref[...] = _mixer_out(x_ref[0], oa_ref[0], sa_ref[0], mb_ref[0], wpa_ref, wout_ref)
    for s in range(SUBLANES):
        for c in range(n_slabs):
            perm_ref[c, pl.ds(s, run, stride=SUBLANES), :] = x1_ref[s * run:(s + 1) * run,
                                                                    c * LANES:(c + 1) * LANES]
    xp = jnp.concatenate([perm_ref[c] for c in range(n_slabs)], axis=1)
    h = _rmsnorm(xp, nf_ref[...]).astype(BF16)
    sublane = lax.broadcasted_iota(jnp.int32, (SUBLANES, 1), 0)
    half = FF_CHUNK // 2
    for c in range(wup_ref.shape[1] // FF_CHUNK):
        cols = slice(c * FF_CHUNK, (c + 1) * FF_CHUNK)
        a = jnp.dot(h, wup_ref[:, cols], preferred_element_type=F32)
        wrap_prev = jnp.where(sublane == 0, halo_ref[pl.ds(halo_row, 1), cols],
                              pltpu.roll(a[-SUBLANES:], 1, 0))
        wrap_next = jnp.where(sublane == SUBLANES - 1, halo_ref[pl.ds(halo_row + 1, 1), cols],
                              pltpu.roll(a[:SUBLANES], SUBLANES - 1, 0))
        prev = jnp.concatenate([wrap_prev, a[:-SUBLANES]], axis=0)
        nxt = jnp.concatenate([a[SUBLANES:], wrap_next], axis=0)
        cw = cw_ref[:, cols].astype(BF16)
        conv = (prev.astype(BF16) * cw[0:1] + a.astype(BF16) * cw[1:2] + nxt.astype(BF16) * cw[2:3]
                + cb_ref[:, cols].astype(BF16))
        act_ref[:, c * half:(c + 1) * half] = _gelu_tanh_exp(conv[:, :half]) * conv[:, half:]
    y = jnp.dot(act_ref[...], wdown_ref[...], preferred_element_type=F32)
    for c in range(n_slabs):
        perm_ref[c] = y[:, c * LANES:(c + 1) * LANES]
    for s in range(SUBLANES):
        rows = slice(s * run, (s + 1) * run)
        out = x1_ref[rows, :] + jnp.concatenate(
            [perm_ref[c, pl.ds(s, run, stride=SUBLANES), :] for c in range(n_slabs)], axis=1)
        if final_norm:
            out = _rmsnorm(out, nfin_ref[...])
        out_ref[0, rows, :] = out


def _conv_ffn(x, oa, sa, mb, wpa, wout, nf, wup, cw, cb, wdown, nfin, final_norm):
    b, s, d_model = x.shape
    tm = min(ROW_TILE, s)
    d_ff = wdown.shape[0]
    assert s % tm == 0 and (2 * d_ff) % FF_CHUNK == 0 and tm % (SUBLANES * SUBLANES) == 0
    nt = s // tm
    assert (b * nt * 2) % SUBLANES == 0
    halo = _ffn_halo((x, oa, sa, mb), wpa, wout, nf, wup, tm)

    def rows(width):
        return pl.BlockSpec((1, tm, width), lambda bi, ti: (bi, ti, 0))

    consts = (wpa, wout, nf, wup, cw, cb, wdown, nfin)
    return pl.pallas_call(
        functools.partial(_conv_ffn_body, final_norm=final_norm),
        grid=(b, s // tm),
        in_specs=[rows(d_model), rows(oa.shape[2]), rows(d_model), rows(d_model),
                  pl.BlockSpec((SUBLANES, 2 * d_ff), lambda bi, ti: ((bi * nt + ti) // (SUBLANES // 2), 0))]
                 + [_const_spec(a.shape) for a in consts],
        out_specs=rows(d_model),
        out_shape=jax.ShapeDtypeStruct(x.shape, F32),
        scratch_shapes=[pltpu.VMEM((tm, d_model), F32), pltpu.VMEM((d_model // LANES, tm, LANES), F32),
                        pltpu.VMEM((tm, d_ff), BF16)],
        compiler_params=_params(2),
        name="conv_ffn",
    )(x, oa, sa, mb, halo, *consts)


def _rel_bucket(rel):
    nb = NUM_BUCKETS // 2
    max_exact = nb // 2
    ret = np.where(rel > 0, nb, 0)
    n = np.abs(rel)
    nf = np.maximum(n, 1).astype(np.float32)
    large = max_exact + (np.log(nf / max_exact) / math.log(MAX_DISTANCE / max_exact)
                         * (nb - max_exact)).astype(np.int32)
    large = np.minimum(large, nb - 1)
    return (ret + np.where(n < max_exact, n, large)).astype(np.int32)


def _score_bias(rel_bias):
    period = Q_BLOCK + KEY_BLOCK
    rel = np.arange(period) - (Q_BLOCK - 1) - HALF_WINDOW
    in_band = jnp.asarray(np.abs(rel) <= HALF_WINDOW)
    col = np.arange(KEY_BLOCK)
    col_masks = [np.zeros(KEY_BLOCK, bool), col < HALF_WINDOW, col >= Q_BLOCK + HALF_WINDOW]
    col_masks.append(col_masks[1] | col_masks[2])
    out = []
    for g, (_, dil) in enumerate(DILATED_GROUPS):
        tab = rel_bias[:, g * HEADS_PER_GROUP:(g + 1) * HEADS_PER_GROUP].astype(F32)
        by_offset = jnp.where(in_band[:, None], tab[jnp.asarray(_rel_bucket(rel * dil))] * LOG2E, MASK_VALUE)
        by_offset = jnp.roll(by_offset.T, -(Q_BLOCK - 1), axis=1)
        bias = jnp.tile(by_offset, (1, Q_BLOCK))[:, :Q_BLOCK * (period - 1)]
        bias = bias.reshape(HEADS_PER_GROUP, Q_BLOCK, period - 1)[:, :, :KEY_BLOCK]
        bias = bias.reshape(HEADS_PER_GROUP // 2, 2 * Q_BLOCK, KEY_BLOCK)
        out.append(jnp.stack([jnp.where(jnp.asarray(cm), MASK_VALUE, bias) for cm in col_masks]))
    return out


def _layer_params(l, norm_mix, w_in, v_gain, w_s, b_s, w_proj_a, w_proj_b, w_out, norm_ffn, w_up,
                  conv_w, conv_b, w_down):
    attn_w = N_GROUPS * GROUP_WIDTH
    width = v_gain.shape[1]
    d_ff = w_down.shape[1]
    w = w_in[l].astype(BF16)
    wq = [jnp.concatenate([w[:, part * attn_w + g * GROUP_WIDTH: part * attn_w + (g + 1) * GROUP_WIDTH]
                           for part in range(3)], axis=1) for g in range(N_GROUPS)]
    wuv = w[:, 3 * attn_w: 3 * attn_w + 2 * width]
    wg = w[:, 3 * attn_w + 2 * width:]
    zb = jnp.repeat(b_s[l].T.astype(F32), width // SG_GROUPS, axis=1)

    def pair_columns(a):
        half = FF_CHUNK // 2
        lead = a.shape[:-1]
        return a.reshape(lead + (2, d_ff // half, half)).swapaxes(-3, -2).reshape(lead + (2 * d_ff,))

    return dict(
        nm=norm_mix[l][None, :], wq=wq, wuv=wuv, wg=wg, vgain=v_gain[l][None, :],
        ws=w_s[l].astype(BF16), zb=zb, wpb=w_proj_b[l].astype(BF16),
        wpa=w_proj_a[l].astype(BF16), wout=w_out[l].astype(BF16),
        nf=norm_ffn[l][None, :], wup=pair_columns(w_up[l].astype(BF16)), cw=pair_columns(conv_w[l]),
        cb=pair_columns(conv_b[l][None, :]), wdown=w_down[l].astype(BF16),
    )


def _trunk(x, layers, biases, nfin):
    seq_len = x.shape[1]
    for i, p in enumerate(layers):
        *qkvs, sa, mb = _mixer_in(x, p["nm"], p["wq"], p["wuv"], p["wg"], p["vgain"], p["ws"], p["zb"],
                                  p["wpb"])
        oa = _attention(qkvs, biases, seq_len)
        x = _conv_ffn(x, oa, sa, mb, p["wpa"], p["wout"], p["nf"], p["wup"], p["cw"], p["cb"], p["wdown"],
                      nfin, final_norm=(i == len(layers) - 1))
    return x


def kernel(x_prompt, x_sample, rel_bias, norm_mix, w_in, v_gain, w_s, b_s, w_proj_a, w_proj_b, w_out,
           norm_ffn, w_up, conv_w, conv_b, w_down, norm_final):
    depth = w_in.shape[0]
    layers = [_layer_params(l, norm_mix, w_in, v_gain, w_s, b_s, w_proj_a, w_proj_b, w_out, norm_ffn,
                            w_up, conv_w, conv_b, w_down) for l in range(depth)]
    biases = _score_bias(rel_bias)
    nfin = norm_final[None, :]
    return (_trunk(x_prompt, layers, biases, nfin), _trunk(x_sample, layers, biases, nfin))
```

```python
import functools
import math

import jax
import jax.numpy as jnp
import numpy as np
from jax import lax
from jax.experimental import pallas as pl
from jax.experimental.pallas import tpu as pltpu

F32 = jnp.float32
BF16 = jnp.bfloat16

HEAD_DIM = 64
HEADS_PER_GROUP = 4
GROUP_WIDTH = HEADS_PER_GROUP * HEAD_DIM
DILATED_GROUPS = ((128, 1), (512, 4), (2048, 16))
N_GROUPS = len(DILATED_GROUPS)
HALF_WINDOW = 64
NUM_BUCKETS = 32
MAX_DISTANCE = 1024
SG_CHUNK = 128
SG_GROUPS = 8
CONV_WIDTH = 3
EPS = 1e-6
MASK_VALUE = -1e30
LOG2E = math.log2(math.e)

LANES = 128
SUBLANES = 8
BF16_ROWS = 16
VMEM_LIMIT_BYTES = 56 * 1024 * 1024

ROW_TILE = 512
ATTN_TILE = 2048
Q_BLOCK = 128
KEY_BLOCK = Q_BLOCK + 2 * HALF_WINDOW
ATTN_UNROLL = 8
FF_CHUNK = 2 * LANES
MERGE_ROWS = 256


def _rmsnorm(x, gain):
    y = x * lax.rsqrt(jnp.mean(x * x, axis=-1, keepdims=True) + EPS)
    return y * gain


def _gelu_tanh(x):
    c = math.sqrt(2.0 / math.pi)
    return 0.5 * x * (1.0 + jnp.tanh(c * (x + 0.044715 * (x * x * x))))


def _gelu_tanh_exp(x):
    k0 = -2.0 * math.sqrt(2.0 / math.pi) * LOG2E
    return x / (1.0 + jnp.exp2(x * (k0 + (0.044715 * k0) * (x * x))))


def _sigmoid(x):
    return 0.5 * jnp.tanh(0.5 * x) + 0.5


def _const_spec(shape):
    zeros = (0,) * len(shape)
    return pl.BlockSpec(shape, lambda *_: zeros, pipeline_mode=pl.Buffered(1))


def _params(n_grid_axes):
    return pltpu.CompilerParams(
        dimension_semantics=("arbitrary",) * n_grid_axes,
        vmem_limit_bytes=VMEM_LIMIT_BYTES,
    )


def _mixer_in_body(x_ref, nm_ref, wq0_ref, wq1_ref, wq2_ref, wuv_ref, wg_ref, vgain_ref, ws_ref,
                   zb_ref, wpb_ref, qkv0_ref, qkv1_ref, qkv2_ref, sa_ref, mb_ref, perm_ref):
    tm = x_ref.shape[1]
    width = vgain_ref.shape[1]
    h = _rmsnorm(x_ref[0], nm_ref[...]).astype(BF16)

    uv = jnp.dot(h, wuv_ref[...], preferred_element_type=F32)
    gates = jnp.dot(h, wg_ref[...], preferred_element_type=F32)

    col = lax.broadcasted_iota(jnp.int32, (1, 3 * GROUP_WIDTH), 1)
    qscale = jnp.where(col < GROUP_WIDTH, LOG2E * HEAD_DIM ** -0.5, 1.0).astype(F32)
    for (_, dil), w_ref, out_ref in zip(DILATED_GROUPS, (wq0_ref, wq1_ref, wq2_ref),
                                        (qkv0_ref, qkv1_ref, qkv2_ref)):
        r = jnp.dot(h, w_ref[...], preferred_element_type=F32) * qscale
        if dil == 1:
            out_ref[0, 0] = r.astype(BF16)
        else:
            for c in range(perm_ref.shape[0]):
                perm_ref[c] = r[:, c * LANES:(c + 1) * LANES]
            for res in range(dil):
                for c in range(perm_ref.shape[0]):
                    out_ref[0, res, :, c * LANES:(c + 1) * LANES] = (
                        perm_ref[c, pl.ds(res, tm // dil, stride=dil), :].astype(BF16))

    d_model = gates.shape[1] // 2
    sig = _sigmoid(gates.astype(BF16))
    sa_ref[0] = sig[:, :d_model]

    uv = _gelu_tanh(uv.astype(BF16))
    u = uv[:, :width]
    vn = _rmsnorm(uv[:, width:].astype(F32), vgain_ref[...]).astype(BF16)
    gch = width // SG_GROUPS
    n_chunks = tm // SG_CHUNK
    mixed = []
    for g in range(SG_GROUPS):
        blocks = jnp.concatenate([vn[c * SG_CHUNK:(c + 1) * SG_CHUNK, g * gch:(g + 1) * gch]
                                  for c in range(n_chunks)], axis=1)
        mixed.append(jnp.dot(ws_ref[g], blocks, preferred_element_type=F32))
    z = jnp.concatenate(
        [jnp.concatenate([mixed[g][:, c * gch:(c + 1) * gch] for g in range(SG_GROUPS)], axis=1) + zb_ref[...]
         for c in range(n_chunks)], axis=0)
    y_b = jnp.dot(u * z.astype(BF16), wpb_ref[...], preferred_element_type=F32)
    mb_ref[0] = sig[:, d_model:] * y_b.astype(BF16)


def _mixer_in(x, nm, wq, wuv, wg, vgain, ws, zb, wpb):
    b, s, d_model = x.shape
    tm = min(ROW_TILE, s)
    assert s % tm == 0 and tm % SG_CHUNK == 0
    max_dil = max(d for _, d in DILATED_GROUPS)
    assert tm % (max_dil * BF16_ROWS) == 0
    qkv_w = 3 * GROUP_WIDTH
    out_shape = [jax.ShapeDtypeStruct((b, dil, s // dil, qkv_w), BF16) for _, dil in DILATED_GROUPS]
    out_shape += [jax.ShapeDtypeStruct((b, s, d_model), BF16)] * 2
    out_specs = [pl.BlockSpec((1, dil, tm // dil, qkv_w), lambda bi, ti: (bi, 0, ti, 0))
                 for _, dil in DILATED_GROUPS]
    out_specs += [pl.BlockSpec((1, tm, d_model), lambda bi, ti: (bi, ti, 0))] * 2
    in_specs = [pl.BlockSpec((1, tm, d_model), lambda bi, ti: (bi, ti, 0)), _const_spec(nm.shape)]
    in_specs += [_const_spec(w.shape) for w in wq]
    in_specs += [_const_spec(a.shape) for a in (wuv, wg, vgain, ws, zb, wpb)]
    return pl.pallas_call(
        _mixer_in_body,
        grid=(b, s // tm),
        in_specs=in_specs,
        out_specs=out_specs,
        out_shape=out_shape,
        scratch_shapes=[pltpu.VMEM((qkv_w // LANES, tm, LANES), F32)],
        compiler_params=_params(2),
        name="mixer_in",
    )(x, nm, *wq, wuv, wg, vgain, ws, zb, wpb)


def _attend_pair(q_pair, k_pair, v_pair, bias, low_half):
    zero = jnp.zeros_like(q_pair)
    q_stack = jnp.concatenate([jnp.where(low_half, q_pair, zero), jnp.where(low_half, zero, q_pair)],
                              axis=0)
    s = lax.dot_general(q_stack, k_pair, (((1,), (1,)), ((), ())), preferred_element_type=F32)
    s = s + bias
    m = jnp.max(s, axis=-1, keepdims=True)
    p = jnp.exp2(s - m)
    denom = jnp.sum(p, axis=-1, keepdims=True)
    pv = jnp.dot(p.astype(BF16), v_pair, preferred_element_type=F32)
    denom = jnp.where(low_half, denom[:Q_BLOCK], denom[Q_BLOCK:])
    out = jnp.where(low_half, pv[:Q_BLOCK], pv[Q_BLOCK:]) * (1.0 / denom)
    return out, jnp.where(low_half, m[:Q_BLOCK], m[Q_BLOCK:]) + jnp.log2(denom)


def _attention_body(*refs):
    n_in = 8 * N_GROUPS
    in_refs, out_ref, scratch = refs[:n_in], refs[n_in], refs[n_in + 1:]
    ti = pl.program_id(1)
    n_tiles = pl.num_programs(1)
    tq = out_ref.shape[1]
    lane = lax.broadcasted_iota(jnp.int32, (1, LANES), 1)
    low_half = lane < HEAD_DIM

    for g, (_, dil) in enumerate(DILATED_GROUPS):
        (q_ref, k_ref, kp_ref, kn_ref, v_ref, vp_ref, vn_ref, bias_ref) = in_refs[8 * g:8 * g + 8]
        kx_ref, vx_ref, o_ref, lse_ref = scratch[4 * g:4 * g + 4]
        tl = tq // dil
        n_blocks = tl // Q_BLOCK

        kx_ref[:, :HALF_WINDOW, :] = kp_ref[0]
        kx_ref[:, HALF_WINDOW:HALF_WINDOW + tl, :] = k_ref[0]
        kx_ref[:, HALF_WINDOW + tl:, :] = kn_ref[0]
        vx_ref[:, :HALF_WINDOW, :] = vp_ref[0]
        vx_ref[:, HALF_WINDOW:HALF_WINDOW + tl, :] = v_ref[0]
        vx_ref[:, HALF_WINDOW + tl:, :] = vn_ref[0]

        def blocks(step, carry, *, dil=dil, n_blocks=n_blocks, q_ref=q_ref, kx_ref=kx_ref,
                   vx_ref=vx_ref, bias_ref=bias_ref, o_ref=o_ref, lse_ref=lse_ref):
            for u in range(ATTN_UNROLL):
                it = step * ATTN_UNROLL + u
                res = it // n_blocks
                jb = it % n_blocks
                row0 = pl.multiple_of(jb * Q_BLOCK, Q_BLOCK)
                q_blk = q_ref[0, res, pl.ds(row0, Q_BLOCK), :]
                k_win = kx_ref[res, pl.ds(row0, KEY_BLOCK), :]
                v_win = vx_ref[res, pl.ds(row0, KEY_BLOCK), :]
                at_start = jnp.logical_and(ti == 0, jb == 0)
                at_end = jnp.logical_and(ti == n_tiles - 1, jb == n_blocks - 1)
                variant = at_start.astype(jnp.int32) + 2 * at_end.astype(jnp.int32)
                outs, lses = [], []
                for hp in range(HEADS_PER_GROUP // 2):
                    cols = slice(hp * LANES, (hp + 1) * LANES)
                    o_pair, lse_pair = _attend_pair(q_blk[:, cols], k_win[:, cols], v_win[:, cols],
                                                    bias_ref[variant, hp], low_half)
                    outs.append(o_pair)
                    lses.append(lse_pair)
                if dil == 1:
                    o_ref[pl.ds(row0, Q_BLOCK), :] = jnp.concatenate(outs, axis=1)
                    lse_ref[pl.ds(row0, Q_BLOCK), :] = jnp.concatenate(lses, axis=1)
                else:
                    start = row0 * dil + res
                    for c in range(GROUP_WIDTH // LANES):
                        o_ref[c, pl.ds(start, Q_BLOCK, stride=dil), :] = outs[c]
                        lse_ref[c, pl.ds(start, Q_BLOCK, stride=dil), :] = lses[c]
            return carry

        lax.fori_loop(0, dil * n_blocks // ATTN_UNROLL, blocks, 0)

    o0_ref, l0_ref = scratch[2], scratch[3]
    o1_ref, l1_ref = scratch[6], scratch[7]
    o2_ref, l2_ref = scratch[10], scratch[11]

    def merge(it, carry):
        row0 = pl.multiple_of(it * MERGE_ROWS, MERGE_ROWS)
        rows = pl.ds(row0, MERGE_ROWS)
        for c in range(GROUP_WIDTH // LANES):
            cols = slice(c * LANES, (c + 1) * LANES)
            l0, l1, l2 = l0_ref[rows, cols], l1_ref[c, rows, :], l2_ref[c, rows, :]
            mx = jnp.maximum(jnp.maximum(l0, l1), l2)
            w0, w1, w2 = jnp.exp2(l0 - mx), jnp.exp2(l1 - mx), jnp.exp2(l2 - mx)
            num = w0 * o0_ref[rows, cols] + w1 * o1_ref[c, rows, :] + w2 * o2_ref[c, rows, :]
            out_ref[0, rows, cols] = (num / (w0 + w1 + w2)).astype(BF16)
        return carry

    lax.fori_loop(0, tq // MERGE_ROWS, merge, 0)


def _attention(qkvs, biases, seq_len):
    b = qkvs[0].shape[0]
    tq = min(ATTN_TILE, seq_len)
    assert seq_len % tq == 0
    in_specs, args, scratch = [], [], []
    for (_, dil), qkv, bias in zip(DILATED_GROUPS, qkvs, biases):
        tl = tq // dil
        assert tl % Q_BLOCK == 0 and (tq // Q_BLOCK) % ATTN_UNROLL == 0
        halo_per_tile = tl // HALF_WINDOW
        n_halo = seq_len // dil // HALF_WINDOW

        def main(col):
            return pl.BlockSpec((1, dil, tl, GROUP_WIDTH), lambda bi, ti, col=col: (bi, 0, ti, col))

        def before(col, hpt=halo_per_tile):
            return pl.BlockSpec((1, dil, HALF_WINDOW, GROUP_WIDTH),
                                lambda bi, ti, col=col: (bi, 0, jnp.maximum(ti * hpt - 1, 0), col))

        def after(col, hpt=halo_per_tile, n_halo=n_halo):
            return pl.BlockSpec((1, dil, HALF_WINDOW, GROUP_WIDTH),
                                lambda bi, ti, col=col: (bi, 0, jnp.minimum((ti + 1) * hpt, n_halo - 1), col))

        in_specs += [main(0), main(1), before(1), after(1), main(2), before(2), after(2),
                     _const_spec(bias.shape)]
        args += [qkv] * 7 + [bias]
        ext = (dil, tl + 2 * HALF_WINDOW, GROUP_WIDTH)
        if dil == 1:
            out_scr = (tq, GROUP_WIDTH)
        else:
            out_scr = (GROUP_WIDTH // LANES, tq, LANES)
        scratch += [pltpu.VMEM(ext, BF16), pltpu.VMEM(ext, BF16),
                    pltpu.VMEM(out_scr, F32), pltpu.VMEM(out_scr, F32)]
    return pl.pallas_call(
        _attention_body,
        grid=(b, seq_len // tq),
        in_specs=in_specs,
        out_specs=pl.BlockSpec((1, tq, GROUP_WIDTH), lambda bi, ti: (bi, ti, 0)),
        out_shape=jax.ShapeDtypeStruct((b, seq_len, GROUP_WIDTH), BF16),
        scratch_shapes=scratch,
        compiler_params=_params(2),
        name="attention",
    )(*args)


def _mixer_out(x, oa, sa, mb, wpa_ref, wout_ref):
    y_a = jnp.dot(oa, wpa_ref[...], preferred_element_type=F32)
    merged = sa.astype(F32) * y_a + mb.astype(F32)
    return x + jnp.dot(merged.astype(BF16), wout_ref[...], preferred_element_type=F32)


def _ffn_halo_body(x_ref, oa_ref, sa_ref, mb_ref, wpa_ref, wout_ref, nf_ref, wup_ref, out_ref):
    x1 = _mixer_out(x_ref[...], oa_ref[...], sa_ref[...], mb_ref[...], wpa_ref, wout_ref)
    h = _rmsnorm(x1, nf_ref[...]).astype(BF16)
    out_ref[...] = jnp.dot(h, wup_ref[...], preferred_element_type=F32)


def _ffn_halo(rows_of, wpa, wout, nf, wup, tm):
    b, s, _ = rows_of[0].shape
    nt = s // tm

    def edge_rows(a):
        at = a.reshape(b, nt, tm, a.shape[2])
        zero = jnp.zeros((b, 1, a.shape[2]), a.dtype)
        before = jnp.concatenate([zero, at[:, :-1, tm - 1, :]], axis=1)
        after = jnp.concatenate([at[:, 1:, 0, :], zero], axis=1)
        return jnp.stack([before, after], axis=2).reshape(b * nt * 2, a.shape[2])

    return pl.pallas_call(
        _ffn_halo_body,
        out_shape=jax.ShapeDtypeStruct((b * nt * 2, wup.shape[1]), F32),
        compiler_params=pltpu.CompilerParams(vmem_limit_bytes=VMEM_LIMIT_BYTES),
        name="ffn_halo",
    )(*[edge_rows(a) for a in rows_of], wpa, wout, nf, wup)


def _conv_ffn_body(x_ref, oa_ref, sa_ref, mb_ref, halo_ref, wpa_ref, wout_ref, nf_ref, wup_ref, cw_ref,
                   cb_ref, wdown_ref, nfin_ref, out_ref, x1_ref, perm_ref, act_ref, *, final_norm):
    tm = x_ref.shape[1]
    n_slabs = perm_ref.shape[0]
    run = tm // SUBLANES
    tile = pl.program_id(0) * pl.num_programs(1) + pl.program_id(1)
    halo_row = (tile % (SUBLANES // 2)) * 2
    x1_ref[...] = _mixer_out(x_ref[0], oa_ref[0], sa_ref[0], mb_ref[0], wpa_ref, wout_ref)
    for s in range(SUBLANES):
        for c in range(n_slabs):
            perm_ref[c, pl.ds(s, run, stride=SUBLANES), :] = x1_ref[s * run:(s + 1) * run,
                                                                    c * LANES:(c + 1) * LANES]
    xp = jnp.concatenate([perm_ref[c] for c in range(n_slabs)], axis=1)
    h = _rmsnorm(xp, nf_ref[...]).astype(BF16)
    sublane = lax.broadcasted_iota(jnp.int32, (SUBLANES, 1), 0)
    half = FF_CHUNK // 2
    for c in range(wup_ref.shape[1] // FF_CHUNK):
        cols = slice(c * FF_CHUNK, (c + 1) * FF_CHUNK)
        a = jnp.dot(h, wup_ref[:, cols], preferred_element_type=F32)
        wrap_prev = jnp.where(sublane == 0, halo_ref[pl.ds(halo_row, 1), cols],
                              pltpu.roll(a[-SUBLANES:], 1, 0))
        wrap_next = jnp.where(sublane == SUBLANES - 1, halo_ref[pl.ds(halo_row + 1, 1), cols],
                              pltpu.roll(a[:SUBLANES], SUBLANES - 1, 0))
        prev = jnp.concatenate([wrap_prev, a[:-SUBLANES]], axis=0)
        nxt = jnp.concatenate([a[SUBLANES:], wrap_next], axis=0)
        cw = cw_ref[:, cols].astype(BF16)
        conv = (prev.astype(BF16) * cw[0:1] + a.astype(BF16) * cw[1:2] + nxt.astype(BF16) * cw[2:3]
                + cb_ref[:, cols].astype(BF16))
        act_ref[:, c * half:(c + 1) * half] = _gelu_tanh_exp(conv[:, :half]) * conv[:, half:]
    y = jnp.dot(act_ref[...], wdown_ref[...], preferred_element_type=F32)
    for c in range(n_slabs):
        perm_ref[c] = y[:, c * LANES:(c + 1) * LANES]
    for s in range(SUBLANES):
        rows = slice(s * run, (s + 1) * run)
        out = x1_ref[rows, :] + jnp.concatenate(
            [perm_ref[c, pl.ds(s, run, stride=SUBLANES), :] for c in range(n_slabs)], axis=1)
        if final_norm:
            out = _rmsnorm(out, nfin_ref[...])
        out_ref[0, rows, :] = out


def _conv_ffn(x, oa, sa, mb, wpa, wout, nf, wup, cw, cb, wdown, nfin, final_norm):
    b, s, d_model = x.shape
    tm = min(ROW_TILE, s)
    d_ff = wdown.shape[0]
    assert s % tm == 0 and (2 * d_ff) % FF_CHUNK == 0 and tm % (SUBLANES * SUBLANES) == 0
    nt = s // tm
    assert (b * nt * 2) % SUBLANES == 0
    halo = _ffn_halo((x, oa, sa, mb), wpa, wout, nf, wup, tm)

    def rows(width):
        return pl.BlockSpec((1, tm, width), lambda bi, ti: (bi, ti, 0))

    consts = (wpa, wout, nf, wup, cw, cb, wdown, nfin)
    return pl.pallas_call(
        functools.partial(_conv_ffn_body, final_norm=final_norm),
        grid=(b, s // tm),
        in_specs=[rows(d_model), rows(oa.shape[2]), rows(d_model), rows(d_model),
                  pl.BlockSpec((SUBLANES, 2 * d_ff), lambda bi, ti: ((bi * nt + ti) // (SUBLANES // 2), 0))]
                 + [_const_spec(a.shape) for a in consts],
        out_specs=rows(d_model),
        out_shape=jax.ShapeDtypeStruct(x.shape, F32),
        scratch_shapes=[pltpu.VMEM((tm, d_model), F32), pltpu.VMEM((d_model // LANES, tm, LANES), F32),
                        pltpu.VMEM((tm, d_ff), BF16)],
        compiler_params=_params(2),
        name="conv_ffn",
    )(x, oa, sa, mb, halo, *consts)


def _rel_bucket(rel):
    nb = NUM_BUCKETS // 2
    max_exact = nb // 2
    ret = np.where(rel > 0, nb, 0)
    n = np.abs(rel)
    nf = np.maximum(n, 1).astype(np.float32)
    large = max_exact + (np.log(nf / max_exact) / math.log(MAX_DISTANCE / max_exact)
                         * (nb - max_exact)).astype(np.int32)
    large = np.minimum(large, nb - 1)
    return (ret + np.where(n < max_exact, n, large)).astype(np.int32)


def _score_bias(rel_bias):
    period = Q_BLOCK + KEY_BLOCK
    rel = np.arange(period) - (Q_BLOCK - 1) - HALF_WINDOW
    in_band = jnp.asarray(np.abs(rel) <= HALF_WINDOW)
    col = np.arange(KEY_BLOCK)
    col_masks = [np.zeros(KEY_BLOCK, bool), col < HALF_WINDOW, col >= Q_BLOCK + HALF_WINDOW]
    col_masks.append(col_masks[1] | col_masks[2])
    out = []
    for g, (_, dil) in enumerate(DILATED_GROUPS):
        tab = rel_bias[:, g * HEADS_PER_GROUP:(g + 1) * HEADS_PER_GROUP].astype(F32)
        by_offset = jnp.where(in_band[:, None], tab[jnp.asarray(_rel_bucket(rel * dil))] * LOG2E, MASK_VALUE)
        by_offset = jnp.roll(by_offset.T, -(Q_BLOCK - 1), axis=1)
        bias = jnp.tile(by_offset, (1, Q_BLOCK))[:, :Q_BLOCK * (period - 1)]
        bias = bias.reshape(HEADS_PER_GROUP, Q_BLOCK, period - 1)[:, :, :KEY_BLOCK]
        bias = bias.reshape(HEADS_PER_GROUP // 2, 2 * Q_BLOCK, KEY_BLOCK)
        out.append(jnp.stack([jnp.where(jnp.asarray(cm), MASK_VALUE, bias) for cm in col_masks]))
    return out


def _layer_params(l, norm_mix, w_in, v_gain, w_s, b_s, w_proj_a, w_proj_b, w_out, norm_ffn, w_up,
                  conv_w, conv_b, w_down):
    attn_w = N_GROUPS * GROUP_WIDTH
    width = v_gain.shape[1]
    d_ff = w_down.shape[1]
    w = w_in[l].astype(BF16)
    wq = [jnp.concatenate([w[:, part * attn_w + g * GROUP_WIDTH: part * attn_w + (g + 1) * GROUP_WIDTH]
                           for part in range(3)], axis=1) for g in range(N_GROUPS)]
    wuv = w[:, 3 * attn_w: 3 * attn_w + 2 * width]
    wg = w[:, 3 * attn_w + 2 * width:]
    zb = jnp.repeat(b_s[l].T.astype(F32), width // SG_GROUPS, axis=1)

    def pair_columns(a):
        half = FF_CHUNK // 2
        return jnp.concatenate([a[..., base + c * half: base + (c + 1) * half]
                                for c in range(d_ff // half) for base in (0, d_ff)], axis=-1)

    return dict(
        nm=norm_mix[l][None, :], wq=wq, wuv=wuv, wg=wg, vgain=v_gain[l][None, :],
        ws=w_s[l].astype(BF16), zb=zb, wpb=w_proj_b[l].astype(BF16),
        wpa=w_proj_a[l].astype(BF16), wout=w_out[l].astype(BF16),
        nf=norm_ffn[l][None, :], wup=pair_columns(w_up[l].astype(BF16)), cw=pair_columns(conv_w[l]),
        cb=pair_columns(conv_b[l][None, :]), wdown=w_down[l].astype(BF16),
    )


def _trunk(x, layers, biases, nfin):
    seq_len = x.shape[1]
    for i, p in enumerate(layers):
        *qkvs, sa, mb = _mixer_in(x, p["nm"], p["wq"], p["wuv"], p["wg"], p["vgain"], p["ws"], p["zb"],
                                  p["wpb"])
        oa = _attention(qkvs, biases, seq_len)
        x = _conv_ffn(x, oa, sa, mb, p["wpa"], p["wout"], p["nf"], p["wup"], p["cw"], p["cb"], p["wdown"],
                      nfin, final_norm=(i == len(layers) - 1))
    return x


def kernel(x_prompt, x_sample, rel_bias, norm_mix, w_in, v_gain, w_s, b_s, w_proj_a, w_proj_b, w_out,
           norm_ffn, w_up, conv_w, conv_b, w_down, norm_final):
    depth = w_in.shape[0]
    layers = [_layer_params(l, norm_mix, w_in, v_gain, w_s, b_s, w_proj_a, w_proj_b, w_out, norm_ffn,
                            w_up, conv_w, conv_b, w_down) for l in range(depth)]
    biases = _score_bias(rel_bias)
    nfin = norm_final[None, :]
    return (_trunk(x_prompt, layers, biases, nfin), _trunk(x_sample, layers, biases, nfin))
```

```python
import functools
import math

import jax
import jax.numpy as jnp
import numpy as np
from jax import lax
from jax.experimental import pallas as pl
from jax.experimental.pallas import tpu as pltpu

F32 = jnp.float32
BF16 = jnp.bfloat16

HEAD_DIM = 64
HEADS_PER_GROUP = 4
GROUP_WIDTH = HEADS_PER_GROUP * HEAD_DIM
DILATED_GROUPS = ((128, 1), (512, 4), (2048, 16))
N_GROUPS = len(DILATED_GROUPS)
HALF_WINDOW = 64
NUM_BUCKETS = 32
MAX_DISTANCE = 1024
SG_CHUNK = 128
SG_GROUPS = 8
CONV_WIDTH = 3
EPS = 1e-6
MASK_VALUE = -1e30
LOG2E = math.log2(math.e)

LANES = 128
SUBLANES = 8
BF16_ROWS = 16
VMEM_LIMIT_BYTES = 56 * 1024 * 1024

ROW_TILE = 512
ATTN_TILE = 2048
Q_BLOCK = 128
KEY_BLOCK = Q_BLOCK + 2 * HALF_WINDOW
ATTN_UNROLL = 16
FF_CHUNK = 2 * LANES
MERGE_ROWS = 256


def _rmsnorm(x, gain):
    y = x * lax.rsqrt(jnp.mean(x * x, axis=-1, keepdims=True) + EPS)
    return y * gain


def _gelu_tanh(x):
    c = math.sqrt(2.0 / math.pi)
    return 0.5 * x * (1.0 + jnp.tanh(c * (x + 0.044715 * (x * x * x))))


def _gelu_tanh_exp(x):
    k0 = -2.0 * math.sqrt(2.0 / math.pi) * LOG2E
    return x / (1.0 + jnp.exp2(x * (k0 + (0.044715 * k0) * (x * x))))


def _sigmoid(x):
    return 0.5 * jnp.tanh(0.5 * x) + 0.5


def _const_spec(shape):
    zeros = (0,) * len(shape)
    return pl.BlockSpec(shape, lambda *_: zeros, pipeline_mode=pl.Buffered(1))


def _params(n_grid_axes):
    return pltpu.CompilerParams(
        dimension_semantics=("arbitrary",) * n_grid_axes,
        vmem_limit_bytes=VMEM_LIMIT_BYTES,
    )


def _mixer_in_body(x_ref, nm_ref, wq0_ref, wq1_ref, wq2_ref, wuv_ref, wg_ref, vgain_ref, ws_ref,
                   zb_ref, wpb_ref, qkv0_ref, qkv1_ref, qkv2_ref, sa_ref, mb_ref, perm_ref):
    tm = x_ref.shape[1]
    width = vgain_ref.shape[1]
    h = _rmsnorm(x_ref[0], nm_ref[...]).astype(BF16)

    uv = jnp.dot(h, wuv_ref[...], preferred_element_type=F32)
    gates = jnp.dot(h, wg_ref[...], preferred_element_type=F32)

    for (_, dil), w_ref, out_ref in zip(DILATED_GROUPS, (wq0_ref, wq1_ref, wq2_ref),
                                        (qkv0_ref, qkv1_ref, qkv2_ref)):
        r = jnp.dot(h, w_ref[...], preferred_element_type=F32)
        r = jnp.concatenate([r[:, :GROUP_WIDTH] * (LOG2E * HEAD_DIM ** -0.5), r[:, GROUP_WIDTH:]], axis=1)
        if dil == 1:
            out_ref[0, 0] = r.astype(BF16)
        else:
            for c in range(perm_ref.shape[0]):
                perm_ref[c] = r[:, c * LANES:(c + 1) * LANES]
            for res in range(dil):
                for c in range(perm_ref.shape[0]):
                    out_ref[0, res, :, c * LANES:(c + 1) * LANES] = (
                        perm_ref[c, pl.ds(res, tm // dil, stride=dil), :].astype(BF16))

    d_model = gates.shape[1] // 2
    sig = _sigmoid(gates.astype(BF16))
    sa_ref[0] = sig[:, :d_model]

    uv = _gelu_tanh(uv.astype(BF16))
    u = uv[:, :width]
    vn = _rmsnorm(uv[:, width:].astype(F32), vgain_ref[...]).astype(BF16)
    gch = width // SG_GROUPS
    n_chunks = tm // SG_CHUNK
    mixed = []
    for g in range(SG_GROUPS):
        blocks = jnp.concatenate([vn[c * SG_CHUNK:(c + 1) * SG_CHUNK, g * gch:(g + 1) * gch]
                                  for c in range(n_chunks)], axis=1)
        mixed.append(jnp.dot(ws_ref[g], blocks, preferred_element_type=F32))
    z = jnp.concatenate(
        [jnp.concatenate([mixed[g][:, c * gch:(c + 1) * gch] for g in range(SG_GROUPS)], axis=1) + zb_ref[...]
         for c in range(n_chunks)], axis=0)
    y_b = jnp.dot(u * z.astype(BF16), wpb_ref[...], preferred_element_type=F32)
    mb_ref[0] = sig[:, d_model:] * y_b.astype(BF16)


def _mixer_in(x, nm, wq, wuv, wg, vgain, ws, zb, wpb):
    b, s, d_model = x.shape
    tm = min(ROW_TILE, s)
    assert s % tm == 0 and tm % SG_CHUNK == 0
    max_dil = max(d for _, d in DILATED_GROUPS)
    assert tm % (max_dil * BF16_ROWS) == 0
    qkv_w = 3 * GROUP_WIDTH
    out_shape = [jax.ShapeDtypeStruct((b, dil, s // dil, qkv_w), BF16) for _, dil in DILATED_GROUPS]
    out_shape += [jax.ShapeDtypeStruct((b, s, d_model), BF16)] * 2
    out_specs = [pl.BlockSpec((1, dil, tm // dil, qkv_w), lambda bi, ti: (bi, 0, ti, 0))
                 for _, dil in DILATED_GROUPS]
    out_specs += [pl.BlockSpec((1, tm, d_model), lambda bi, ti: (bi, ti, 0))] * 2
    in_specs = [pl.BlockSpec((1, tm, d_model), lambda bi, ti: (bi, ti, 0)), _const_spec(nm.shape)]
    in_specs += [_const_spec(w.shape) for w in wq]
    in_specs += [_const_spec(a.shape) for a in (wuv, wg, vgain, ws, zb, wpb)]
    return pl.pallas_call(
        _mixer_in_body,
        grid=(b, s // tm),
        in_specs=in_specs,
        out_specs=out_specs,
        out_shape=out_shape,
        scratch_shapes=[pltpu.VMEM((qkv_w // LANES, tm, LANES), F32)],
        compiler_params=_params(2),
        name="mixer_in",
    )(x, nm, *wq, wuv, wg, vgain, ws, zb, wpb)


def _attend_pair(q_pair, k_pair, v_pair, bias, low_half):
    zero = jnp.zeros_like(q_pair)
    q_stack = jnp.concatenate([jnp.where(low_half, q_pair, zero), jnp.where(low_half, zero, q_pair)],
                              axis=0)
    s = lax.dot_general(q_stack, k_pair, (((1,), (1,)), ((), ())), preferred_element_type=F32)
    s = s + bias
    m = jnp.max(s, axis=-1, keepdims=True)
    p = jnp.exp2(s - m)
    denom = jnp.sum(p, axis=-1, keepdims=True)
    pv = jnp.dot(p.astype(BF16), v_pair, preferred_element_type=F32)
    denom = jnp.where(low_half, denom[:Q_BLOCK], denom[Q_BLOCK:])
    out = jnp.where(low_half, pv[:Q_BLOCK], pv[Q_BLOCK:]) * (1.0 / denom)
    return out, jnp.where(low_half, m[:Q_BLOCK], m[Q_BLOCK:]) + jnp.log2(denom)


def _attention_body(*refs):
    n_in = 8 * N_GROUPS
    in_refs, out_ref, scratch = refs[:n_in], refs[n_in], refs[n_in + 1:]
    ti = pl.program_id(1)
    n_tiles = pl.num_programs(1)
    tq = out_ref.shape[1]
    lane = lax.broadcasted_iota(jnp.int32, (1, LANES), 1)
    low_half = lane < HEAD_DIM

    for g, (_, dil) in enumerate(DILATED_GROUPS):
        (q_ref, k_ref, kp_ref, kn_ref, v_ref, vp_ref, vn_ref, bias_ref) = in_refs[8 * g:8 * g + 8]
        kx_ref, vx_ref, o_ref, lse_ref = scratch[4 * g:4 * g + 4]
        tl = tq // dil
        n_blocks = tl // Q_BLOCK

        kx_ref[:, :HALF_WINDOW, :] = kp_ref[0]
        kx_ref[:, HALF_WINDOW:HALF_WINDOW + tl, :] = k_ref[0]
        kx_ref[:, HALF_WINDOW + tl:, :] = kn_ref[0]
        vx_ref[:, :HALF_WINDOW, :] = vp_ref[0]
        vx_ref[:, HALF_WINDOW:HALF_WINDOW + tl, :] = v_ref[0]
        vx_ref[:, HALF_WINDOW + tl:, :] = vn_ref[0]

        def blocks(step, carry, *, dil=dil, n_blocks=n_blocks, q_ref=q_ref, kx_ref=kx_ref,
                   vx_ref=vx_ref, bias_ref=bias_ref, o_ref=o_ref, lse_ref=lse_ref):
            for u in range(ATTN_UNROLL):
                it = step * ATTN_UNROLL + u
                res = it // n_blocks
                jb = it % n_blocks
                row0 = pl.multiple_of(jb * Q_BLOCK, Q_BLOCK)
                q_blk = q_ref[0, res, pl.ds(row0, Q_BLOCK), :]
                k_win = kx_ref[res, pl.ds(row0, KEY_BLOCK), :]
                v_win = vx_ref[res, pl.ds(row0, KEY_BLOCK), :]
                at_start = jnp.logical_and(ti == 0, jb == 0)
                at_end = jnp.logical_and(ti == n_tiles - 1, jb == n_blocks - 1)
                variant = at_start.astype(jnp.int32) + 2 * at_end.astype(jnp.int32)
                outs, lses = [], []
                for hp in range(HEADS_PER_GROUP // 2):
                    cols = slice(hp * LANES, (hp + 1) * LANES)
                    o_pair, lse_pair = _attend_pair(q_blk[:, cols], k_win[:, cols], v_win[:, cols],
                                                    bias_ref[variant, hp], low_half)
                    outs.append(o_pair)
                    lses.append(lse_pair)
                if dil == 1:
                    o_ref[pl.ds(row0, Q_BLOCK), :] = jnp.concatenate(outs, axis=1)
                    lse_ref[pl.ds(row0, Q_BLOCK), :] = jnp.concatenate(lses, axis=1)
                else:
                    start = row0 * dil + res
                    for c in range(GROUP_WIDTH // LANES):
                        o_ref[c, pl.ds(start, Q_BLOCK, stride=dil), :] = outs[c]
                        lse_ref[c, pl.ds(start, Q_BLOCK, stride=dil), :] = lses[c]
            return carry

        lax.fori_loop(0, dil * n_blocks // ATTN_UNROLL, blocks, 0)

    o0_ref, l0_ref = scratch[2], scratch[3]
    o1_ref, l1_ref = scratch[6], scratch[7]
    o2_ref, l2_ref = scratch[10], scratch[11]

    def merge(it, carry):
        row0 = pl.multiple_of(it * MERGE_ROWS, MERGE_ROWS)
        rows = pl.ds(row0, MERGE_ROWS)
        for c in range(GROUP_WIDTH // LANES):
            cols = slice(c * LANES, (c + 1) * LANES)
            l0, l1, l2 = l0_ref[rows, cols], l1_ref[c, rows, :], l2_ref[c, rows, :]
            mx = jnp.maximum(jnp.maximum(l0, l1), l2)
            w0, w1, w2 = jnp.exp2(l0 - mx), jnp.exp2(l1 - mx), jnp.exp2(l2 - mx)
            num = w0 * o0_ref[rows, cols] + w1 * o1_ref[c, rows, :] + w2 * o2_ref[c, rows, :]
            out_ref[0, rows, cols] = (num / (w0 + w1 + w2)).astype(BF16)
        return carry

    lax.fori_loop(0, tq // MERGE_ROWS, merge, 0)


def _attention(qkvs, biases, seq_len):
    b = qkvs[0].shape[0]
    tq = min(ATTN_TILE, seq_len)
    assert seq_len % tq == 0
    in_specs, args, scratch = [], [], []
    for (_, dil), qkv, bias in zip(DILATED_GROUPS, qkvs, biases):
        tl = tq // dil
        assert tl % Q_BLOCK == 0 and (tq // Q_BLOCK) % ATTN_UNROLL == 0
        halo_per_tile = tl // HALF_WINDOW
        n_halo = seq_len // dil // HALF_WINDOW

        def main(col):
            return pl.BlockSpec((1, dil, tl, GROUP_WIDTH), lambda bi, ti, col=col: (bi, 0, ti, col))

        def before(col, hpt=halo_per_tile):
            return pl.BlockSpec((1, dil, HALF_WINDOW, GROUP_WIDTH),
                                lambda bi, ti, col=col: (bi, 0, jnp.maximum(ti * hpt - 1, 0), col))

        def after(col, hpt=halo_per_tile, n_halo=n_halo):
            return pl.BlockSpec((1, dil, HALF_WINDOW, GROUP_WIDTH),
                                lambda bi, ti, col=col: (bi, 0, jnp.minimum((ti + 1) * hpt, n_halo - 1), col))

        in_specs += [main(0), main(1), before(1), after(1), main(2), before(2), after(2),
                     _const_spec(bias.shape)]
        args += [qkv] * 7 + [bias]
        ext = (dil, tl + 2 * HALF_WINDOW, GROUP_WIDTH)
        if dil == 1:
            out_scr = (tq, GROUP_WIDTH)
        else:
            out_scr = (GROUP_WIDTH // LANES, tq, LANES)
        scratch += [pltpu.VMEM(ext, BF16), pltpu.VMEM(ext, BF16),
                    pltpu.VMEM(out_scr, F32), pltpu.VMEM(out_scr, F32)]
    return pl.pallas_call(
        _attention_body,
        grid=(b, seq_len // tq),
        in_specs=in_specs,
        out_specs=pl.BlockSpec((1, tq, GROUP_WIDTH), lambda bi, ti: (bi, ti, 0)),
        out_shape=jax.ShapeDtypeStruct((b, seq_len, GROUP_WIDTH), BF16),
        scratch_shapes=scratch,
        compiler_params=_params(2),
        name="attention",
    )(*args)


def _mixer_out(x, oa, sa, mb, wpa_ref, wout_ref):
    y_a = jnp.dot(oa, wpa_ref[...], preferred_element_type=F32)
    merged = sa * y_a.astype(BF16) + mb
    return x + jnp.dot(merged, wout_ref[...], preferred_element_type=F32)


def _ffn_halo_body(x_ref, oa_ref, sa_ref, mb_ref, wpa_ref, wout_ref, nf_ref, wup_ref, out_ref):
    x1 = _mixer_out(x_ref[...], oa_ref[...], sa_ref[...], mb_ref[...], wpa_ref, wout_ref)
    h = _rmsnorm(x1, nf_ref[...]).astype(BF16)
    out_ref[...] = jnp.dot(h, wup_ref[...], preferred_element_type=F32)


def _ffn_halo(rows_of, wpa, wout, nf, wup, tm):
    b, s, _ = rows_of[0].shape
    nt = s // tm

    def edge_rows(a):
        at = a.reshape(b, nt, tm, a.shape[2])
        zero = jnp.zeros((b, 1, a.shape[2]), a.dtype)
        before = jnp.concatenate([zero, at[:, :-1, tm - 1, :]], axis=1)
        after = jnp.concatenate([at[:, 1:, 0, :], zero], axis=1)
        return jnp.stack([before, after], axis=2).reshape(b * nt * 2, a.shape[2])

    return pl.pallas_call(
        _ffn_halo_body,
        out_shape=jax.ShapeDtypeStruct((b * nt * 2, wup.shape[1]), F32),
        compiler_params=pltpu.CompilerParams(vmem_limit_bytes=VMEM_LIMIT_BYTES),
        name="ffn_halo",
    )(*[edge_rows(a) for a in rows_of], wpa, wout, nf, wup)


def _conv_ffn_body(x_ref, oa_ref, sa_ref, mb_ref, halo_ref, wpa_ref, wout_ref, nf_ref, wup_ref, cw_ref,
                   cb_ref, wdown_ref, nfin_ref, out_ref, x1_ref, perm_ref, act_ref, *, final_norm):
    tm = x_ref.shape[1]
    n_slabs = perm_ref.shape[0]
    run = tm // SUBLANES
    tile = pl.program_id(0) * pl.num_programs(1) + pl.program_id(1)
    halo_row = (tile % (SUBLANES // 2)) * 2
    x1_ref[...] = _mixer_out(x_ref[0], oa_ref[0], sa_ref[0], mb_ref[0], wpa_ref, wout_ref)
    for s in range(SUBLANES):
        for c in range(n_slabs):
            perm_ref[c, pl.ds(s, run, stride=SUBLANES), :] = x1_ref[s * run:(s + 1) * run,
                                                                    c * LANES:(c + 1) * LANES]
    xp = jnp.concatenate([perm_ref[c] for c in range(n_slabs)], axis=1)
    h = _rmsnorm(xp, nf_ref[...]).astype(BF16)
    sublane = lax.broadcasted_iota(jnp.int32, (SUBLANES, 1), 0)
    half = FF_CHUNK // 2
    for c in range(wup_ref.shape[1] // FF_CHUNK):
        cols = slice(c * FF_CHUNK, (c + 1) * FF_CHUNK)
        a = jnp.dot(h, wup_ref[:, cols], preferred_element_type=F32)
        wrap_prev = jnp.where(sublane == 0, halo_ref[pl.ds(halo_row, 1), cols],
                              pltpu.roll(a[-SUBLANES:], 1, 0))
        wrap_next = jnp.where(sublane == SUBLANES - 1, halo_ref[pl.ds(halo_row + 1, 1), cols],
                              pltpu.roll(a[:SUBLANES], SUBLANES - 1, 0))
        prev = jnp.concatenate([wrap_prev, a[:-SUBLANES]], axis=0)
        nxt = jnp.concatenate([a[SUBLANES:], wrap_next], axis=0)
        cw = cw_ref[:, cols].astype(BF16)
        conv = (prev.astype(BF16) * cw[0:1] + a.astype(BF16) * cw[1:2] + nxt.astype(BF16) * cw[2:3]
                + cb_ref[:, cols].astype(BF16))
        act_ref[:, c * half:(c + 1) * half] = _gelu_tanh_exp(conv[:, :half]) * conv[:, half:]
    y = jnp.dot(act_ref[...], wdown_ref[...], preferred_element_type=F32)
    for c in range(n_slabs):
        perm_ref[c] = y[:, c * LANES:(c + 1) * LANES]
    for s in range(SUBLANES):
        rows = slice(s * run, (s + 1) * run)
        out = x1_ref[rows, :] + jnp.concatenate(
            [perm_ref[c, pl.ds(s, run, stride=SUBLANES), :] for c in range(n_slabs)], axis=1)
        if final_norm:
            out = _rmsnorm(out, nfin_ref[...])
        out_ref[0, rows, :] = out


def _conv_ffn(x, oa, sa, mb, wpa, wout, nf, wup, cw, cb, wdown, nfin, final_norm):
    b, s, d_model = x.shape
    tm = min(ROW_TILE, s)
    d_ff = wdown.shape[0]
    assert s % tm == 0 and (2 * d_ff) % FF_CHUNK == 0 and tm % (SUBLANES * SUBLANES) == 0
    nt = s // tm
    assert (b * nt * 2) % SUBLANES == 0
    halo = _ffn_halo((x, oa, sa, mb), wpa, wout, nf, wup, tm)

    def rows(width):
        return pl.BlockSpec((1, tm, width), lambda bi, ti: (bi, ti, 0))

    consts = (wpa, wout, nf, wup, cw, cb, wdown, nfin)
    return pl.pallas_call(
        functools.partial(_conv_ffn_body, final_norm=final_norm),
        grid=(b, s // tm),
        in_specs=[rows(d_model), rows(oa.shape[2]), rows(d_model), rows(d_model),
                  pl.BlockSpec((SUBLANES, 2 * d_ff), lambda bi, ti: ((bi * nt + ti) // (SUBLANES // 2), 0))]
                 + [_const_spec(a.shape) for a in consts],
        out_specs=rows(d_model),
        out_shape=jax.ShapeDtypeStruct(x.shape, F32),
        scratch_shapes=[pltpu.VMEM((tm, d_model), F32), pltpu.VMEM((d_model // LANES, tm, LANES), F32),
                        pltpu.VMEM((tm, d_ff), BF16)],
        compiler_params=_params(2),
        name="conv_ffn",
    )(x, oa, sa, mb, halo, *consts)


def _rel_bucket(rel):
    nb = NUM_BUCKETS // 2
    max_exact = nb // 2
    ret = np.where(rel > 0, nb, 0)
    n = np.abs(rel)
    nf = np.maximum(n, 1).astype(np.float32)
    large = max_exact + (np.log(nf / max_exact) / math.log(MAX_DISTANCE / max_exact)
                         * (nb - max_exact)).astype(np.int32)
    large = np.minimum(large, nb - 1)
    return (ret + np.where(n < max_exact, n, large)).astype(np.int32)


def _score_bias(rel_bias):
    period = Q_BLOCK + KEY_BLOCK
    rel = np.arange(period) - (Q_BLOCK - 1) - HALF_WINDOW
    in_band = jnp.asarray(np.abs(rel) <= HALF_WINDOW)
    col = np.arange(KEY_BLOCK)
    col_masks = [np.zeros(KEY_BLOCK, bool), col < HALF_WINDOW, col >= Q_BLOCK + HALF_WINDOW]
    col_masks.append(col_masks[1] | col_masks[2])
    out = []
    for g, (_, dil) in enumerate(DILATED_GROUPS):
        tab = rel_bias[:, g * HEADS_PER_GROUP:(g + 1) * HEADS_PER_GROUP].astype(F32)
        by_offset = jnp.where(in_band[:, None], tab[jnp.asarray(_rel_bucket(rel * dil))] * LOG2E, MASK_VALUE)
        by_offset = jnp.roll(by_offset.T, -(Q_BLOCK - 1), axis=1)
        bias = jnp.tile(by_offset, (1, Q_BLOCK))[:, :Q_BLOCK * (period - 1)]
        bias = bias.reshape(HEADS_PER_GROUP, Q_BLOCK, period - 1)[:, :, :KEY_BLOCK]
        bias = bias.reshape(HEADS_PER_GROUP // 2, 2 * Q_BLOCK, KEY_BLOCK)
        out.append(jnp.stack([jnp.where(jnp.asarray(cm), MASK_VALUE, bias) for cm in col_masks]))
    return out


def _layer_params(l, norm_mix, w_in, v_gain, w_s, b_s, w_proj_a, w_proj_b, w_out, norm_ffn, w_up,
                  conv_w, conv_b, w_down):
    attn_w = N_GROUPS * GROUP_WIDTH
    width = v_gain.shape[1]
    d_ff = w_down.shape[1]
    w = w_in[l].astype(BF16)
    wq = [jnp.concatenate([w[:, part * attn_w + g * GROUP_WIDTH: part * attn_w + (g + 1) * GROUP_WIDTH]
                           for part in range(3)], axis=1) for g in range(N_GROUPS)]
    wuv = w[:, 3 * attn_w: 3 * attn_w + 2 * width]
    wg = w[:, 3 * attn_w + 2 * width:]
    zb = jnp.repeat(b_s[l].T.astype(F32), width // SG_GROUPS, axis=1)

    def pair_columns(a):
        half = FF_CHUNK // 2
        return jnp.concatenate([a[..., base + c * half: base + (c + 1) * half]
                                for c in range(d_ff // half) for base in (0, d_ff)], axis=-1)

    return dict(
        nm=norm_mix[l][None, :], wq=wq, wuv=wuv, wg=wg, vgain=v_gain[l][None, :],
        ws=w_s[l].astype(BF16), zb=zb, wpb=w_proj_b[l].astype(BF16),
        wpa=w_proj_a[l].astype(BF16), wout=w_out[l].astype(BF16),
        nf=norm_ffn[l][None, :], wup=pair_columns(w_up[l].astype(BF16)), cw=pair_columns(conv_w[l]),
        cb=pair_columns(conv_b[l][None, :]), wdown=w_down[l].astype(BF16),
    )


def _trunk(x, layers, biases, nfin):
    seq_len = x.shape[1]
    for i, p in enumerate(layers):
        *qkvs, sa, mb = _mixer_in(x, p["nm"], p["wq"], p["wuv"], p["wg"], p["vgain"], p["ws"], p["zb"],
                                  p["wpb"])
        oa = _attention(qkvs, biases, seq_len)
        x = _conv_ffn(x, oa, sa, mb, p["wpa"], p["wout"], p["nf"], p["wup"], p["cw"], p["cb"], p["wdown"],
                      nfin, final_norm=(i == len(layers) - 1))
    return x


def kernel(x_prompt, x_sample, rel_bias, norm_mix, w_in, v_gain, w_s, b_s, w_proj_a, w_proj_b, w_out,
           norm_ffn, w_up, conv_w, conv_b, w_down, norm_final):
    depth = w_in.shape[0]
    layers = [_layer_params(l, norm_mix, w_in, v_gain, w_s, b_s, w_proj_a, w_proj_b, w_out, norm_ffn,
                            w_up, conv_w, conv_b, w_down) for l in range(depth)]
    biases = _score_bias(rel_bias)
    nfin = norm_final[None, :]
    return (_trunk(x_prompt, layers, biases, nfin), _trunk(x_sample, layers, biases, nfin))
```

```python
import functools
import math

import jax
import jax.numpy as jnp
import numpy as np
from jax import lax
from jax.experimental import pallas as pl
from jax.experimental.pallas import tpu as pltpu

F32 = jnp.float32
BF16 = jnp.bfloat16

HEAD_DIM = 64
HEADS_PER_GROUP = 4
GROUP_WIDTH = HEADS_PER_GROUP * HEAD_DIM
DILATED_GROUPS = ((128, 1), (512, 4), (2048, 16))
N_GROUPS = len(DILATED_GROUPS)
HALF_WINDOW = 64
NUM_BUCKETS = 32
MAX_DISTANCE = 1024
SG_CHUNK = 128
SG_GROUPS = 8
CONV_WIDTH = 3
EPS = 1e-6
MASK_VALUE = -1e30
LOG2E = math.log2(math.e)

LANES = 128
SUBLANES = 8
BF16_ROWS = 16
VMEM_LIMIT_BYTES = 56 * 1024 * 1024

ROW_TILE = 512
ATTN_TILE = 2048
Q_BLOCK = 128
KEY_BLOCK = Q_BLOCK + 2 * HALF_WINDOW
FF_CHUNK = 2 * LANES
MERGE_ROWS = 256
DEINTERLEAVE_STRIDE = 4


def _rmsnorm(x, gain):
    y = x * lax.rsqrt(jnp.mean(x * x, axis=-1, keepdims=True) + EPS)
    return y * gain


def _gelu_tanh(x):
    c = math.sqrt(2.0 / math.pi)
    return 0.5 * x * (1.0 + jnp.tanh(c * (x + 0.044715 * (x * x * x))))


def _gelu_tanh_exp(x):
    k0 = -2.0 * math.sqrt(2.0 / math.pi) * LOG2E
    return x / (1.0 + jnp.exp2(x * (k0 + (0.044715 * k0) * (x * x))))


def _sigmoid(x):
    return 0.5 * jnp.tanh(0.5 * x) + 0.5


def _const_spec(shape):
    zeros = (0,) * len(shape)
    return pl.BlockSpec(shape, lambda *_: zeros, pipeline_mode=pl.Buffered(1))


def _params(n_grid_axes):
    return pltpu.CompilerParams(
        dimension_semantics=("arbitrary",) * n_grid_axes,
        vmem_limit_bytes=VMEM_LIMIT_BYTES,
    )


def _mixer_in_body(x_ref, nm_ref, wq0_ref, wq1_ref, wq2_ref, wuv_ref, wg_ref, vgain_ref, ws_ref,
                   zb_ref, wpb_ref, qkv0_ref, qkv1_ref, qkv2_ref, sa_ref, mb_ref, perm_ref, perm2_ref):
    tm = x_ref.shape[1]
    width = vgain_ref.shape[1]
    h = _rmsnorm(x_ref[0], nm_ref[...]).astype(BF16)

    uv = jnp.dot(h, wuv_ref[...], preferred_element_type=F32)
    gates = jnp.dot(h, wg_ref[...], preferred_element_type=F32)

    for (_, dil), w_ref, out_ref in zip(DILATED_GROUPS, (wq0_ref, wq1_ref, wq2_ref),
                                        (qkv0_ref, qkv1_ref, qkv2_ref)):
        r = jnp.dot(h, w_ref[...], preferred_element_type=F32)
        r = jnp.concatenate([r[:, :GROUP_WIDTH] * (LOG2E * HEAD_DIM ** -0.5), r[:, GROUP_WIDTH:]], axis=1)
        if dil == 1:
            out_ref[0, 0] = r.astype(BF16)
            continue
        for c in range(perm_ref.shape[0]):
            perm_ref[c] = r[:, c * LANES:(c + 1) * LANES]
        src_ref, src_dil = perm_ref, dil
        if dil > DEINTERLEAVE_STRIDE:
            n = tm // DEINTERLEAVE_STRIDE
            for c in range(perm_ref.shape[0]):
                for j in range(DEINTERLEAVE_STRIDE):
                    perm2_ref[c, j * n:(j + 1) * n, :] = perm_ref[c, pl.ds(j, n, stride=DEINTERLEAVE_STRIDE), :]
            src_ref, src_dil = perm2_ref, dil // DEINTERLEAVE_STRIDE
        for res in range(dil):
            j, k = res % (dil // src_dil), res // (dil // src_dil)
            start = j * (tm // (dil // src_dil)) + k
            for c in range(perm_ref.shape[0]):
                out_ref[0, res, :, c * LANES:(c + 1) * LANES] = (
                    src_ref[c, pl.ds(start, tm // dil, stride=src_dil), :].astype(BF16))

    d_model = gates.shape[1] // 2
    sig = _sigmoid(gates.astype(BF16))
    sa_ref[0] = sig[:, :d_model]

    uv = _gelu_tanh(uv.astype(BF16))
    u = uv[:, :width]
    vn = _rmsnorm(uv[:, width:].astype(F32), vgain_ref[...]).astype(BF16)
    gch = width // SG_GROUPS
    n_chunks = tm // SG_CHUNK
    mixed = []
    for g in range(SG_GROUPS):
        blocks = jnp.concatenate([vn[c * SG_CHUNK:(c + 1) * SG_CHUNK, g * gch:(g + 1) * gch]
                                  for c in range(n_chunks)], axis=1)
        mixed.append(jnp.dot(ws_ref[g], blocks, preferred_element_type=F32))
    z = jnp.concatenate(
        [jnp.concatenate([mixed[g][:, c * gch:(c + 1) * gch] for g in range(SG_GROUPS)], axis=1) + zb_ref[...]
         for c in range(n_chunks)], axis=0)
    y_b = jnp.dot(u * z.astype(BF16), wpb_ref[...], preferred_element_type=F32)
    mb_ref[0] = sig[:, d_model:] * y_b.astype(BF16)


def _mixer_in(x, nm, wq, wuv, wg, vgain, ws, zb, wpb):
    b, s, d_model = x.shape
    tm = min(ROW_TILE, s)
    assert s % tm == 0 and tm % SG_CHUNK == 0
    max_dil = max(d for _, d in DILATED_GROUPS)
    assert tm % (max_dil * BF16_ROWS) == 0
    qkv_w = 3 * GROUP_WIDTH
    out_shape = [jax.ShapeDtypeStruct((b, dil, s // dil, qkv_w), BF16) for _, dil in DILATED_GROUPS]
    out_shape += [jax.ShapeDtypeStruct((b, s, d_model), BF16)] * 2
    out_specs = [pl.BlockSpec((1, dil, tm // dil, qkv_w), lambda bi, ti: (bi, 0, ti, 0))
                 for _, dil in DILATED_GROUPS]
    out_specs += [pl.BlockSpec((1, tm, d_model), lambda bi, ti: (bi, ti, 0))] * 2
    in_specs = [pl.BlockSpec((1, tm, d_model), lambda bi, ti: (bi, ti, 0)), _const_spec(nm.shape)]
    in_specs += [_const_spec(w.shape) for w in wq]
    in_specs += [_const_spec(a.shape) for a in (wuv, wg, vgain, ws, zb, wpb)]
    return pl.pallas_call(
        _mixer_in_body,
        grid=(b, s // tm),
        in_specs=in_specs,
        out_specs=out_specs,
        out_shape=out_shape,
        scratch_shapes=[pltpu.VMEM((qkv_w // LANES, tm, LANES), F32)] * 2,
        compiler_params=_params(2),
        name="mixer_in",
    )(x, nm, *wq, wuv, wg, vgain, ws, zb, wpb)


def _attend_pair(q_pair, k_pair, v_pair, bias, low_half):
    zero = jnp.zeros_like(q_pair)
    q_stack = jnp.concatenate([jnp.where(low_half, q_pair, zero), jnp.where(low_half, zero, q_pair)],
                              axis=0)
    s = lax.dot_general(q_stack, k_pair, (((1,), (1,)), ((), ())), preferred_element_type=F32)
    s = s + bias
    m = jnp.max(s, axis=-1, keepdims=True)
    p = jnp.exp2(s - m)
    denom = jnp.sum(p, axis=-1, keepdims=True)
    pv = jnp.dot(p.astype(BF16), v_pair, preferred_element_type=F32)
    denom = jnp.where(low_half, denom[:Q_BLOCK], denom[Q_BLOCK:])
    out = jnp.where(low_half, pv[:Q_BLOCK], pv[Q_BLOCK:]) * (1.0 / denom)
    return out, jnp.where(low_half, m[:Q_BLOCK], m[Q_BLOCK:]) + jnp.log2(denom)


def _attention_body(*refs):
    n_in = 8 * N_GROUPS
    in_refs, out_ref, scratch = refs[:n_in], refs[n_in], refs[n_in + 1:]
    ti = pl.program_id(1)
    n_tiles = pl.num_programs(1)
    tq = out_ref.shape[1]
    lane = lax.broadcasted_iota(jnp.int32, (1, LANES), 1)
    low_half = lane < HEAD_DIM

    for g, (_, dil) in enumerate(DILATED_GROUPS):
        (q_ref, k_ref, kp_ref, kn_ref, v_ref, vp_ref, vn_ref, bias_ref) = in_refs[8 * g:8 * g + 8]
        o_ref, lse_ref = scratch[2 * g:2 * g + 2]
        tl = tq // dil
        n_blocks = tl // Q_BLOCK

        def window(main_ref, before_ref, after_ref, res, jb, tl=tl):
            lo, hi = jb * Q_BLOCK - HALF_WINDOW, (jb + 1) * Q_BLOCK + HALF_WINDOW
            parts = [main_ref[0, res, max(lo, 0):min(hi, tl), :]]
            if lo < 0:
                parts.insert(0, before_ref[0, res])
            if hi > tl:
                parts.append(after_ref[0, res])
            return jnp.concatenate(parts, axis=0)

        for unit in range(dil * n_blocks):
            res, jb = divmod(unit, n_blocks)
            row0 = jb * Q_BLOCK
            q_blk = q_ref[0, res, row0:row0 + Q_BLOCK, :]
            k_win = window(k_ref, kp_ref, kn_ref, res, jb)
            v_win = window(v_ref, vp_ref, vn_ref, res, jb)
            variant = 0
            if jb == 0:
                variant = variant + (ti == 0).astype(jnp.int32)
            if jb == n_blocks - 1:
                variant = variant + 2 * (ti == n_tiles - 1).astype(jnp.int32)
            outs, lses = [], []
            for hp in range(HEADS_PER_GROUP // 2):
                cols = slice(hp * LANES, (hp + 1) * LANES)
                o_pair, lse_pair = _attend_pair(q_blk[:, cols], k_win[:, cols], v_win[:, cols],
                                                bias_ref[variant, hp], low_half)
                outs.append(o_pair)
                lses.append(lse_pair)
            if dil == 1:
                o_ref[row0:row0 + Q_BLOCK, :] = jnp.concatenate(outs, axis=1)
                lse_ref[row0:row0 + Q_BLOCK, :] = jnp.concatenate(lses, axis=1)
            else:
                rows = pl.ds(row0 * dil + res, Q_BLOCK, stride=dil)
                for c in range(GROUP_WIDTH // LANES):
                    o_ref[c, rows, :] = outs[c]
                    lse_ref[c, rows, :] = lses[c]

    (o0_ref, l0_ref), (o1_ref, l1_ref), (o2_ref, l2_ref) = (scratch[2 * g:2 * g + 2] for g in range(N_GROUPS))

    def merge(it, carry):
        row0 = pl.multiple_of(it * MERGE_ROWS, MERGE_ROWS)
        rows = pl.ds(row0, MERGE_ROWS)
        for c in range(GROUP_WIDTH // LANES):
            cols = slice(c * LANES, (c + 1) * LANES)
            l0, l1, l2 = l0_ref[rows, cols], l1_ref[c, rows, :], l2_ref[c, rows, :]
            mx = jnp.maximum(jnp.maximum(l0, l1), l2)
            w0, w1, w2 = jnp.exp2(l0 - mx), jnp.exp2(l1 - mx), jnp.exp2(l2 - mx)
            num = w0 * o0_ref[rows, cols] + w1 * o1_ref[c, rows, :] + w2 * o2_ref[c, rows, :]
            out_ref[0, rows, cols] = (num / (w0 + w1 + w2)).astype(BF16)
        return carry

    lax.fori_loop(0, tq // MERGE_ROWS, merge, 0)


def _attention(qkvs, biases, seq_len):
    b = qkvs[0].shape[0]
    tq = min(ATTN_TILE, seq_len)
    assert seq_len % tq == 0
    in_specs, args, scratch = [], [], []
    for (_, dil), qkv, bias in zip(DILATED_GROUPS, qkvs, biases):
        tl = tq // dil
        assert tl % Q_BLOCK == 0
        halo_per_tile = tl // HALF_WINDOW
        n_halo = seq_len // dil // HALF_WINDOW

        def main(col):
            return pl.BlockSpec((1, dil, tl, GROUP_WIDTH), lambda bi, ti, col=col: (bi, 0, ti, col))

        def before(col, hpt=halo_per_tile):
            return pl.BlockSpec((1, dil, HALF_WINDOW, GROUP_WIDTH),
                                lambda bi, ti, col=col: (bi, 0, jnp.maximum(ti * hpt - 1, 0), col))

        def after(col, hpt=halo_per_tile, n_halo=n_halo):
            return pl.BlockSpec((1, dil, HALF_WINDOW, GROUP_WIDTH),
                                lambda bi, ti, col=col: (bi, 0, jnp.minimum((ti + 1) * hpt, n_halo - 1), col))

        in_specs += [main(0), main(1), before(1), after(1), main(2), before(2), after(2),
                     _const_spec(bias.shape)]
        args += [qkv] * 7 + [bias]
        if dil == 1:
            out_scr = (tq, GROUP_WIDTH)
        else:
            out_scr = (GROUP_WIDTH // LANES, tq, LANES)
        scratch += [pltpu.VMEM(out_scr, F32), pltpu.VMEM(out_scr, F32)]
    return pl.pallas_call(
        _attention_body,
        grid=(b, seq_len // tq),
        in_specs=in_specs,
        out_specs=pl.BlockSpec((1, tq, GROUP_WIDTH), lambda bi, ti: (bi, ti, 0)),
        out_shape=jax.ShapeDtypeStruct((b, seq_len, GROUP_WIDTH), BF16),
        scratch_shapes=scratch,
        compiler_params=_params(2),
        name="attention",
    )(*args)


def _mixer_out(x, oa, sa, mb, wpa_ref, wout_ref):
    y_a = jnp.dot(oa, wpa_ref[...], preferred_element_type=F32)
    merged = sa * y_a.astype(BF16) + mb
    return x + jnp.dot(merged, wout_ref[...], preferred_element_type=F32)


def _ffn_halo_body(x_ref, oa_ref, sa_ref, mb_ref, wpa_ref, wout_ref, nf_ref, wup_ref, out_ref):
    x1 = _mixer_out(x_ref[...], oa_ref[...], sa_ref[...], mb_ref[...], wpa_ref, wout_ref)
    h = _rmsnorm(x1, nf_ref[...]).astype(BF16)
    out_ref[...] = jnp.dot(h, wup_ref[...], preferred_element_type=F32)


def _ffn_halo(rows_of, wpa, wout, nf, wup, tm):
    b, s, _ = rows_of[0].shape
    nt = s // tm

    def edge_rows(a):
        at = a.reshape(b, nt, tm, a.shape[2])
        zero = jnp.zeros((b, 1, a.shape[2]), a.dtype)
        before = jnp.concatenate([zero, at[:, :-1, tm - 1, :]], axis=1)
        after = jnp.concatenate([at[:, 1:, 0, :], zero], axis=1)
        return jnp.stack([before, after], axis=2).reshape(b * nt * 2, a.shape[2])

    return pl.pallas_call(
        _ffn_halo_body,
        out_shape=jax.ShapeDtypeStruct((b * nt * 2, wup.shape[1]), F32),
        compiler_params=pltpu.CompilerParams(vmem_limit_bytes=VMEM_LIMIT_BYTES),
        name="ffn_halo",
    )(*[edge_rows(a) for a in rows_of], wpa, wout, nf, wup)


def _conv_ffn_body(x_ref, oa_ref, sa_ref, mb_ref, halo_ref, wpa_ref, wout_ref, nf_ref, wup_ref, cw_ref,
                   cb_ref, wdown_ref, nfin_ref, out_ref, x1_ref, perm_ref, act_ref, *, final_norm):
    tm = x_ref.shape[1]
    n_slabs = perm_ref.shape[0]
    run = tm // SUBLANES
    tile = pl.program_id(0) * pl.num_programs(1) + pl.program_id(1)
    halo_row = (tile % (SUBLANES // 2)) * 2
    x1_ref[...] = _mixer_out(x_ref[0], oa_ref[0], sa_ref[0], mb_ref[0], wpa_ref, wout_ref)
    for s in range(SUBLANES):
        for c in range(n_slabs):
            perm_ref[c, pl.ds(s, run, stride=SUBLANES), :] = x1_ref[s * run:(s + 1) * run,
                                                                    c * LANES:(c + 1) * LANES]
    xp = jnp.concatenate([perm_ref[c] for c in range(n_slabs)], axis=1)
    h = _rmsnorm(xp, nf_ref[...]).astype(BF16)
    sublane = lax.broadcasted_iota(jnp.int32, (SUBLANES, 1), 0)
    half = FF_CHUNK // 2
    for c in range(wup_ref.shape[1] // FF_CHUNK):
        cols = slice(c * FF_CHUNK, (c + 1) * FF_CHUNK)
        a = jnp.dot(h, wup_ref[:, cols], preferred_element_type=F32)
        wrap_prev = jnp.where(sublane == 0, halo_ref[pl.ds(halo_row, 1), cols],
                              pltpu.roll(a[-SUBLANES:], 1, 0))
        wrap_next = jnp.where(sublane == SUBLANES - 1, halo_ref[pl.ds(halo_row + 1, 1), cols],
                              pltpu.roll(a[:SUBLANES], SUBLANES - 1, 0))
        prev = jnp.concatenate([wrap_prev, a[:-SUBLANES]], axis=0)
        nxt = jnp.concatenate([a[SUBLANES:], wrap_next], axis=0)
        cw = cw_ref[:, cols].astype(BF16)
        conv = (prev.astype(BF16) * cw[0:1] + a.astype(BF16) * cw[1:2] + nxt.astype(BF16) * cw[2:3]
                + cb_ref[:, cols].astype(BF16))
        act_ref[:, c * half:(c + 1) * half] = _gelu_tanh_exp(conv[:, :half]) * conv[:, half:]
    y = jnp.dot(act_ref[...], wdown_ref[...], preferred_element_type=F32)
    for c in range(n_slabs):
        perm_ref[c] = y[:, c * LANES:(c + 1) * LANES]
    for s in range(SUBLANES):
        rows = slice(s * run, (s + 1) * run)
        out = x1_ref[rows, :] + jnp.concatenate(
            [perm_ref[c, pl.ds(s, run, stride=SUBLANES), :] for c in range(n_slabs)], axis=1)
        if final_norm:
            out = _rmsnorm(out, nfin_ref[...])
        out_ref[0, rows, :] = out


def _conv_ffn(x, oa, sa, mb, wpa, wout, nf, wup, cw, cb, wdown, nfin, final_norm):
    b, s, d_model = x.shape
    tm = min(ROW_TILE, s)
    d_ff = wdown.shape[0]
    assert s % tm == 0 and (2 * d_ff) % FF_CHUNK == 0 and tm % (SUBLANES * SUBLANES) == 0
    nt = s // tm
    assert (b * nt * 2) % SUBLANES == 0
    halo = _ffn_halo((x, oa, sa, mb), wpa, wout, nf, wup, tm)

    def rows(width):
        return pl.BlockSpec((1, tm, width), lambda bi, ti: (bi, ti, 0))

    consts = (wpa, wout, nf, wup, cw, cb, wdown, nfin)
    return pl.pallas_call(
        functools.partial(_conv_ffn_body, final_norm=final_norm),
        grid=(b, s // tm),
        in_specs=[rows(d_model), rows(oa.shape[2]), rows(d_model), rows(d_model),
                  pl.BlockSpec((SUBLANES, 2 * d_ff), lambda bi, ti: ((bi * nt + ti) // (SUBLANES // 2), 0))]
                 + [_const_spec(a.shape) for a in consts],
        out_specs=rows(d_model),
        out_shape=jax.ShapeDtypeStruct(x.shape, F32),
        scratch_shapes=[pltpu.VMEM((tm, d_model), F32), pltpu.VMEM((d_model // LANES, tm, LANES), F32),
                        pltpu.VMEM((tm, d_ff), BF16)],
        compiler_params=_params(2),
        name="conv_ffn",
    )(x, oa, sa, mb, halo, *consts)


def _rel_bucket(rel):
    nb = NUM_BUCKETS // 2
    max_exact = nb // 2
    ret = np.where(rel > 0, nb, 0)
    n = np.abs(rel)
    nf = np.maximum(n, 1).astype(np.float32)
    large = max_exact + (np.log(nf / max_exact) / math.log(MAX_DISTANCE / max_exact)
                         * (nb - max_exact)).astype(np.int32)
    large = np.minimum(large, nb - 1)
    return (ret + np.where(n < max_exact, n, large)).astype(np.int32)


def _score_bias(rel_bias):
    period = Q_BLOCK + KEY_BLOCK
    rel = np.arange(period) - (Q_BLOCK - 1) - HALF_WINDOW
    in_band = jnp.asarray(np.abs(rel) <= HALF_WINDOW)
    col = np.arange(KEY_BLOCK)
    col_masks = [np.zeros(KEY_BLOCK, bool), col < HALF_WINDOW, col >= Q_BLOCK + HALF_WINDOW]
    col_masks.append(col_masks[1] | col_masks[2])
    out = []
    for g, (_, dil) in enumerate(DILATED_GROUPS):
        tab = rel_bias[:, g * HEADS_PER_GROUP:(g + 1) * HEADS_PER_GROUP].astype(F32)
        by_offset = jnp.where(in_band[:, None], tab[jnp.asarray(_rel_bucket(rel * dil))] * LOG2E, MASK_VALUE)
        by_offset = jnp.roll(by_offset.T, -(Q_BLOCK - 1), axis=1)
        bias = jnp.tile(by_offset, (1, Q_BLOCK))[:, :Q_BLOCK * (period - 1)]
        bias = bias.reshape(HEADS_PER_GROUP, Q_BLOCK, period - 1)[:, :, :KEY_BLOCK]
        bias = bias.reshape(HEADS_PER_GROUP // 2, 2 * Q_BLOCK, KEY_BLOCK)
        out.append(jnp.stack([jnp.where(jnp.asarray(cm), MASK_VALUE, bias) for cm in col_masks]))
    return out


def _layer_params(l, norm_mix, w_in, v_gain, w_s, b_s, w_proj_a, w_proj_b, w_out, norm_ffn, w_up,
                  conv_w, conv_b, w_down):
    attn_w = N_GROUPS * GROUP_WIDTH
    width = v_gain.shape[1]
    d_ff = w_down.shape[1]
    w = w_in[l].astype(BF16)
    wq = [jnp.concatenate([w[:, part * attn_w + g * GROUP_WIDTH: part * attn_w + (g + 1) * GROUP_WIDTH]
                           for part in range(3)], axis=1) for g in range(N_GROUPS)]
    wuv = w[:, 3 * attn_w: 3 * attn_w + 2 * width]
    wg = w[:, 3 * attn_w + 2 * width:]
    zb = jnp.repeat(b_s[l].T.astype(F32), width // SG_GROUPS, axis=1)

    def pair_columns(a):
        half = FF_CHUNK // 2
        return jnp.concatenate([a[..., base + c * half: base + (c + 1) * half]
                                for c in range(d_ff // half) for base in (0, d_ff)], axis=-1)

    return dict(
        nm=norm_mix[l][None, :], wq=wq, wuv=wuv, wg=wg, vgain=v_gain[l][None, :],
        ws=w_s[l].astype(BF16), zb=zb, wpb=w_proj_b[l].astype(BF16),
        wpa=w_proj_a[l].astype(BF16), wout=w_out[l].astype(BF16),
        nf=norm_ffn[l][None, :], wup=pair_columns(w_up[l].astype(BF16)), cw=pair_columns(conv_w[l]),
        cb=pair_columns(conv_b[l][None, :]), wdown=w_down[l].astype(BF16),
    )


def _trunk(x, layers, biases, nfin):
    seq_len = x.shape[1]
    for i, p in enumerate(layers):
        *qkvs, sa, mb = _mixer_in(x, p["nm"], p["wq"], p["wuv"], p["wg"], p["vgain"], p["ws"], p["zb"],
                                  p["wpb"])
        oa = _attention(qkvs, biases, seq_len)
        x = _conv_ffn(x, oa, sa, mb, p["wpa"], p["wout"], p["nf"], p["wup"], p["cw"], p["cb"], p["wdown"],
                      nfin, final_norm=(i == len(layers) - 1))
    return x


def kernel(x_prompt, x_sample, rel_bias, norm_mix, w_in, v_gain, w_s, b_s, w_proj_a, w_proj_b, w_out,
           norm_ffn, w_up, conv_w, conv_b, w_down, norm_final):
    depth = w_in.shape[0]
    layers = [_layer_params(l, norm_mix, w_in, v_gain, w_s, b_s, w_proj_a, w_proj_b, w_out, norm_ffn,
                            w_up, conv_w, conv_b, w_down) for l in range(depth)]
    biases = _score_bias(rel_bias)
    nfin = norm_final[None, :]
    return (_trunk(x_prompt, layers, biases, nfin), _trunk(x_sample, layers, biases, nfin))
```

```python
import functools
import math

import jax
import jax.numpy as jnp
import numpy as np
from jax import lax
from jax.experimental import pallas as pl
from jax.experimental.pallas import tpu as pltpu

F32 = jnp.float32
BF16 = jnp.bfloat16

HEAD_DIM = 64
HEADS_PER_GROUP = 4
GROUP_WIDTH = HEADS_PER_GROUP * HEAD_DIM
DILATED_GROUPS = ((128, 1), (512, 4), (2048, 16))
N_GROUPS = len(DILATED_GROUPS)
HALF_WINDOW = 64
NUM_BUCKETS = 32
MAX_DISTANCE = 1024
SG_CHUNK = 128
SG_GROUPS = 8
CONV_WIDTH = 3
EPS = 1e-6
MASK_VALUE = -1e30
LOG2E = math.log2(math.e)

LANES = 128
SUBLANES = 8
BF16_ROWS = 16
VMEM_LIMIT_BYTES = 56 * 1024 * 1024

ROW_TILE = 512
ATTN_TILE = 2048
Q_BLOCK = 128
KEY_BLOCK = Q_BLOCK + 2 * HALF_WINDOW
FF_CHUNK = 2 * LANES
MERGE_ROWS = 256
DEINTERLEAVE_STRIDE = 4


def _rmsnorm(x, gain):
    y = x * lax.rsqrt(jnp.mean(x * x, axis=-1, keepdims=True) + EPS)
    return y * gain


def _gelu_tanh(x):
    c = math.sqrt(2.0 / math.pi)
    return 0.5 * x * (1.0 + jnp.tanh(c * (x + 0.044715 * (x * x * x))))


def _gelu_tanh_exp(x):
    k0 = -2.0 * math.sqrt(2.0 / math.pi) * LOG2E
    return x / (1.0 + jnp.exp2(x * (k0 + (0.044715 * k0) * (x * x))))


def _sigmoid(x):
    return 0.5 * jnp.tanh(0.5 * x) + 0.5


def _const_spec(shape):
    zeros = (0,) * len(shape)
    return pl.BlockSpec(shape, lambda *_: zeros, pipeline_mode=pl.Buffered(1))


def _params(n_grid_axes):
    return pltpu.CompilerParams(
        dimension_semantics=("arbitrary",) * n_grid_axes,
        vmem_limit_bytes=VMEM_LIMIT_BYTES,
    )


def _mixer_in_body(x_ref, nm_ref, wq0_ref, wq1_ref, wq2_ref, wuv_ref, wg_ref, vgain_ref, ws_ref,
                   zb_ref, wpb_ref, qkv0_ref, qkv1_ref, qkv2_ref, sa_ref, mb_ref, perm_ref, perm2_ref):
    tm = x_ref.shape[1]
    width = vgain_ref.shape[1]
    h = _rmsnorm(x_ref[0], nm_ref[...]).astype(BF16)

    uv = jnp.dot(h, wuv_ref[...], preferred_element_type=F32)
    gates = jnp.dot(h, wg_ref[...], preferred_element_type=F32)

    for (_, dil), w_ref, out_ref in zip(DILATED_GROUPS, (wq0_ref, wq1_ref, wq2_ref),
                                        (qkv0_ref, qkv1_ref, qkv2_ref)):
        r = jnp.dot(h, w_ref[...], preferred_element_type=F32)
        r = jnp.concatenate([r[:, :GROUP_WIDTH] * (LOG2E * HEAD_DIM ** -0.5), r[:, GROUP_WIDTH:]], axis=1)
        if dil == 1:
            out_ref[0, 0] = r.astype(BF16)
            continue
        for c in range(perm_ref.shape[0]):
            perm_ref[c] = r[:, c * LANES:(c + 1) * LANES]
        src_ref, src_dil = perm_ref, dil
        if dil > DEINTERLEAVE_STRIDE:
            n = tm // DEINTERLEAVE_STRIDE
            for c in range(perm_ref.shape[0]):
                for j in range(DEINTERLEAVE_STRIDE):
                    perm2_ref[c, j * n:(j + 1) * n, :] = perm_ref[c, pl.ds(j, n, stride=DEINTERLEAVE_STRIDE), :]
            src_ref, src_dil = perm2_ref, dil // DEINTERLEAVE_STRIDE
        for res in range(dil):
            j, k = res % (dil // src_dil), res // (dil // src_dil)
            start = j * (tm // (dil // src_dil)) + k
            for c in range(perm_ref.shape[0]):
                out_ref[0, res, :, c * LANES:(c + 1) * LANES] = (
                    src_ref[c, pl.ds(start, tm // dil, stride=src_dil), :].astype(BF16))

    d_model = gates.shape[1] // 2
    sig = _sigmoid(gates.astype(BF16))
    sa_ref[0] = sig[:, :d_model]

    uv = _gelu_tanh(uv.astype(BF16))
    u = uv[:, :width]
    vn = _rmsnorm(uv[:, width:].astype(F32), vgain_ref[...]).astype(BF16)
    gch = width // SG_GROUPS
    n_chunks = tm // SG_CHUNK
    mixed = []
    for g in range(SG_GROUPS):
        blocks = jnp.concatenate([vn[c * SG_CHUNK:(c + 1) * SG_CHUNK, g * gch:(g + 1) * gch]
                                  for c in range(n_chunks)], axis=1)
        mixed.append(jnp.dot(ws_ref[g], blocks, preferred_element_type=F32))
    z = jnp.concatenate(
        [jnp.concatenate([mixed[g][:, c * gch:(c + 1) * gch] for g in range(SG_GROUPS)], axis=1) + zb_ref[...]
         for c in range(n_chunks)], axis=0)
    y_b = jnp.dot(u * z.astype(BF16), wpb_ref[...], preferred_element_type=F32)
    mb_ref[0] = sig[:, d_model:] * y_b.astype(BF16)


def _mixer_in(x, nm, wq, wuv, wg, vgain, ws, zb, wpb):
    b, s, d_model = x.shape
    tm = min(ROW_TILE, s)
    assert s % tm == 0 and tm % SG_CHUNK == 0
    max_dil = max(d for _, d in DILATED_GROUPS)
    assert tm % (max_dil * BF16_ROWS) == 0
    qkv_w = 3 * GROUP_WIDTH
    out_shape = [jax.ShapeDtypeStruct((b, dil, s // dil, qkv_w), BF16) for _, dil in DILATED_GROUPS]
    out_shape += [jax.ShapeDtypeStruct((b, s, d_model), BF16)] * 2
    out_specs = [pl.BlockSpec((1, dil, tm // dil, qkv_w), lambda bi, ti: (bi, 0, ti, 0))
                 for _, dil in DILATED_GROUPS]
    out_specs += [pl.BlockSpec((1, tm, d_model), lambda bi, ti: (bi, ti, 0))] * 2
    in_specs = [pl.BlockSpec((1, tm, d_model), lambda bi, ti: (bi, ti, 0)), _const_spec(nm.shape)]
    in_specs += [_const_spec(w.shape) for w in wq]
    in_specs += [_const_spec(a.shape) for a in (wuv, wg, vgain, ws, zb, wpb)]
    return pl.pallas_call(
        _mixer_in_body,
        grid=(b, s // tm),
        in_specs=in_specs,
        out_specs=out_specs,
        out_shape=out_shape,
        scratch_shapes=[pltpu.VMEM((qkv_w // LANES, tm, LANES), F32)] * 2,
        compiler_params=_params(2),
        name="mixer_in",
    )(x, nm, *wq, wuv, wg, vgain, ws, zb, wpb)


def _attend_pair(q_pair, k_pair, v_pair, bias, low_half):
    zero = jnp.zeros_like(q_pair)
    q_stack = jnp.concatenate([jnp.where(low_half, q_pair, zero), jnp.where(low_half, zero, q_pair)],
                              axis=0)
    s = lax.dot_general(q_stack, k_pair, (((1,), (1,)), ((), ())), preferred_element_type=F32)
    s = s + bias
    m = jnp.max(s, axis=-1, keepdims=True)
    p = jnp.exp2(s - m)
    denom = jnp.sum(p, axis=-1, keepdims=True)
    pv = jnp.dot(p.astype(BF16), v_pair, preferred_element_type=F32)
    denom = jnp.where(low_half, denom[:Q_BLOCK], denom[Q_BLOCK:])
    out = jnp.where(low_half, pv[:Q_BLOCK], pv[Q_BLOCK:]) * (1.0 / denom)
    return out, jnp.where(low_half, m[:Q_BLOCK], m[Q_BLOCK:]) + jnp.log2(denom)


def _attention_body(*refs):
    n_in = 8 * N_GROUPS
    in_refs, out_ref, scratch = refs[:n_in], refs[n_in], refs[n_in + 1:]
    ti = pl.program_id(1)
    n_tiles = pl.num_programs(1)
    tq = out_ref.shape[1]
    lane = lax.broadcasted_iota(jnp.int32, (1, LANES), 1)
    low_half = lane < HEAD_DIM

    for g, (_, dil) in enumerate(DILATED_GROUPS):
        (q_ref, k_ref, kp_ref, kn_ref, v_ref, vp_ref, vn_ref, bias_ref) = in_refs[8 * g:8 * g + 8]
        o_ref, lse_ref = scratch[2 * g:2 * g + 2]
        tl = tq // dil
        n_blocks = tl // Q_BLOCK

        def window(main_ref, before_ref, after_ref, res, jb, tl=tl):
            lo, hi = jb * Q_BLOCK - HALF_WINDOW, (jb + 1) * Q_BLOCK + HALF_WINDOW
            parts = [main_ref[0, res, max(lo, 0):min(hi, tl), :]]
            if lo < 0:
                parts.insert(0, before_ref[0, res])
            if hi > tl:
                parts.append(after_ref[0, res])
            return jnp.concatenate(parts, axis=0)

        for unit in range(dil * n_blocks):
            res, jb = divmod(unit, n_blocks)
            row0 = jb * Q_BLOCK
            q_blk = q_ref[0, res, row0:row0 + Q_BLOCK, :]
            k_win = window(k_ref, kp_ref, kn_ref, res, jb)
            v_win = window(v_ref, vp_ref, vn_ref, res, jb)
            variant = 0
            if jb == 0:
                variant = variant + (ti == 0).astype(jnp.int32)
            if jb == n_blocks - 1:
                variant = variant + 2 * (ti == n_tiles - 1).astype(jnp.int32)
            outs, lses = [], []
            for hp in range(HEADS_PER_GROUP // 2):
                cols = slice(hp * LANES, (hp + 1) * LANES)
                o_pair, lse_pair = _attend_pair(q_blk[:, cols], k_win[:, cols], v_win[:, cols],
                                                bias_ref[variant, hp], low_half)
                outs.append(o_pair)
                lses.append(lse_pair)
            if dil == 1:
                o_ref[row0:row0 + Q_BLOCK, :] = jnp.concatenate(outs, axis=1)
                lse_ref[row0:row0 + Q_BLOCK, :] = jnp.concatenate(lses, axis=1)
            else:
                rows = pl.ds(row0 * dil + res, Q_BLOCK, stride=dil)
                for c in range(GROUP_WIDTH // LANES):
                    o_ref[c, rows, :] = outs[c]
                    lse_ref[c, rows, :] = lses[c]

    (o0_ref, l0_ref), (o1_ref, l1_ref), (o2_ref, l2_ref) = (scratch[2 * g:2 * g + 2] for g in range(N_GROUPS))

    def merge(it, carry):
        row0 = pl.multiple_of(it * MERGE_ROWS, MERGE_ROWS)
        rows = pl.ds(row0, MERGE_ROWS)
        for c in range(GROUP_WIDTH // LANES):
            cols = slice(c * LANES, (c + 1) * LANES)
            l0, l1, l2 = l0_ref[rows, cols], l1_ref[c, rows, :], l2_ref[c, rows, :]
            mx = jnp.maximum(jnp.maximum(l0, l1), l2)
            w0, w1, w2 = jnp.exp2(l0 - mx), jnp.exp2(l1 - mx), jnp.exp2(l2 - mx)
            num = w0 * o0_ref[rows, cols] + w1 * o1_ref[c, rows, :] + w2 * o2_ref[c, rows, :]
            out_ref[0, rows, cols] = (num / (w0 + w1 + w2)).astype(BF16)
        return carry

    lax.fori_loop(0, tq // MERGE_ROWS, merge, 0)


def _attention(qkvs, biases, seq_len):
    b = qkvs[0].shape[0]
    tq = min(ATTN_TILE, seq_len)
    assert seq_len % tq == 0
    in_specs, args, scratch = [], [], []
    for (_, dil), qkv, bias in zip(DILATED_GROUPS, qkvs, biases):
        tl = tq // dil
        assert tl % Q_BLOCK == 0
        halo_per_tile = tl // HALF_WINDOW
        n_halo = seq_len // dil // HALF_WINDOW

        def main(col):
            return pl.BlockSpec((1, dil, tl, GROUP_WIDTH), lambda bi, ti, col=col: (bi, 0, ti, col))

        def before(col, hpt=halo_per_tile):
            return pl.BlockSpec((1, dil, HALF_WINDOW, GROUP_WIDTH),
                                lambda bi, ti, col=col: (bi, 0, jnp.maximum(ti * hpt - 1, 0), col))

        def after(col, hpt=halo_per_tile, n_halo=n_halo):
            return pl.BlockSpec((1, dil, HALF_WINDOW, GROUP_WIDTH),
                                lambda bi, ti, col=col: (bi, 0, jnp.minimum((ti + 1) * hpt, n_halo - 1), col))

        in_specs += [main(0), main(1), before(1), after(1), main(2), before(2), after(2),
                     _const_spec(bias.shape)]
        args += [qkv] * 7 + [bias]
        if dil == 1:
            out_scr = (tq, GROUP_WIDTH)
        else:
            out_scr = (GROUP_WIDTH // LANES, tq, LANES)
        scratch += [pltpu.VMEM(out_scr, F32), pltpu.VMEM(out_scr, F32)]
    return pl.pallas_call(
        _attention_body,
        grid=(b, seq_len // tq),
        in_specs=in_specs,
        out_specs=pl.BlockSpec((1, tq, GROUP_WIDTH), lambda bi, ti: (bi, ti, 0)),
        out_shape=jax.ShapeDtypeStruct((b, seq_len, GROUP_WIDTH), BF16),
        scratch_shapes=scratch,
        compiler_params=_params(2),
        name="attention",
    )(*args)


def _mixer_out(x, oa, sa, mb, wpa_ref, wout_ref):
    y_a = jnp.dot(oa, wpa_ref[...], preferred_element_type=F32)
    merged = sa * y_a.astype(BF16) + mb
    return x + jnp.dot(merged, wout_ref[...], preferred_element_type=F32)


def _ffn_halo_body(x_ref, oa_ref, sa_ref, mb_ref, wpa_ref, wout_ref, nf_ref, wup_ref, out_ref):
    x1 = _mixer_out(x_ref[...], oa_ref[...], sa_ref[...], mb_ref[...], wpa_ref, wout_ref)
    h = _rmsnorm(x1, nf_ref[...]).astype(BF16)
    out_ref[...] = jnp.dot(h, wup_ref[...], preferred_element_type=F32)


def _ffn_halo(rows_of, wpa, wout, nf, wup, tm):
    b, s, _ = rows_of[0].shape
    nt = s // tm

    def edge_rows(a):
        at = a.reshape(b, nt, tm, a.shape[2])
        zero = jnp.zeros((b, 1, a.shape[2]), a.dtype)
        before = jnp.concatenate([zero, at[:, :-1, tm - 1, :]], axis=1)
        after = jnp.concatenate([at[:, 1:, 0, :], zero], axis=1)
        return jnp.stack([before, after], axis=2).reshape(b * nt * 2, a.shape[2])

    return pl.pallas_call(
        _ffn_halo_body,
        out_shape=jax.ShapeDtypeStruct((b * nt * 2, wup.shape[1]), F32),
        compiler_params=pltpu.CompilerParams(vmem_limit_bytes=VMEM_LIMIT_BYTES),
        name="ffn_halo",
    )(*[edge_rows(a) for a in rows_of], wpa, wout, nf, wup)


def _conv_ffn_body(x_ref, oa_ref, sa_ref, mb_ref, halo_ref, wpa_ref, wout_ref, nf_ref, wup_ref, cw_ref,
                   cb_ref, wdown_ref, nfin_ref, out_ref, x1_ref, perm_ref, perm2_ref, act_ref, *, final_norm):
    tm = x_ref.shape[1]
    n_slabs = perm_ref.shape[0]
    quarter = tm // DEINTERLEAVE_STRIDE
    tile = pl.program_id(0) * pl.num_programs(1) + pl.program_id(1)
    halo_row = (tile % (SUBLANES // 2)) * 2
    x1_ref[...] = _mixer_out(x_ref[0], oa_ref[0], sa_ref[0], mb_ref[0], wpa_ref, wout_ref)
    for q in range(DEINTERLEAVE_STRIDE):
        for c in range(n_slabs):
            perm2_ref[c, pl.ds(q, quarter, stride=DEINTERLEAVE_STRIDE), :] = (
                x1_ref[q * quarter:(q + 1) * quarter, c * LANES:(c + 1) * LANES])
    for q in range(DEINTERLEAVE_STRIDE):
        for c in range(n_slabs):
            perm_ref[c, pl.ds(q, quarter, stride=DEINTERLEAVE_STRIDE), :] = perm2_ref[c, q * quarter:(q + 1) * quarter, :]
    xp = jnp.concatenate([perm_ref[c] for c in range(n_slabs)], axis=1)
    h = _rmsnorm(xp, nf_ref[...]).astype(BF16)
    row = lax.broadcasted_iota(jnp.int32, (BF16_ROWS, 1), 0)
    half = FF_CHUNK // 2
    for c in range(wup_ref.shape[1] // FF_CHUNK):
        cols = slice(c * FF_CHUNK, (c + 1) * FF_CHUNK)
        a = jnp.dot(h, wup_ref[:, cols], preferred_element_type=F32)
        a16 = a.astype(BF16)
        wrap_prev = jnp.where(row == 0, halo_ref[pl.ds(halo_row, 1), cols],
                              pltpu.roll(a[-BF16_ROWS:], 1, 0)).astype(BF16)
        wrap_next = jnp.where(row == BF16_ROWS - 1, halo_ref[pl.ds(halo_row + 1, 1), cols],
                              pltpu.roll(a[:BF16_ROWS], BF16_ROWS - 1, 0)).astype(BF16)
        prev = jnp.concatenate([wrap_prev, a16[:-BF16_ROWS]], axis=0)
        nxt = jnp.concatenate([a16[BF16_ROWS:], wrap_next], axis=0)
        cw = cw_ref[:, cols].astype(BF16)
        conv = prev * cw[0:1] + a16 * cw[1:2] + nxt * cw[2:3] + cb_ref[:, cols].astype(BF16)
        act_ref[:, c * half:(c + 1) * half] = _gelu_tanh_exp(conv[:, :half]) * conv[:, half:]
    y = jnp.dot(act_ref[...], wdown_ref[...], preferred_element_type=F32)
    for c in range(n_slabs):
        perm_ref[c] = y[:, c * LANES:(c + 1) * LANES]
    for q in range(DEINTERLEAVE_STRIDE):
        for c in range(n_slabs):
            perm2_ref[c, q * quarter:(q + 1) * quarter, :] = perm_ref[c, pl.ds(q, quarter, stride=DEINTERLEAVE_STRIDE), :]
    for q in range(DEINTERLEAVE_STRIDE):
        rows = slice(q * quarter, (q + 1) * quarter)
        out = x1_ref[rows, :] + jnp.concatenate(
            [perm2_ref[c, pl.ds(q, quarter, stride=DEINTERLEAVE_STRIDE), :] for c in range(n_slabs)], axis=1)
        if final_norm:
            out = _rmsnorm(out, nfin_ref[...])
        out_ref[0, rows, :] = out


def _conv_ffn(x, oa, sa, mb, wpa, wout, nf, wup, cw, cb, wdown, nfin, final_norm):
    b, s, d_model = x.shape
    tm = min(ROW_TILE, s)
    d_ff = wdown.shape[0]
    assert s % tm == 0 and (2 * d_ff) % FF_CHUNK == 0 and tm % (BF16_ROWS * BF16_ROWS) == 0
    nt = s // tm
    assert (b * nt * 2) % SUBLANES == 0
    halo = _ffn_halo((x, oa, sa, mb), wpa, wout, nf, wup, tm)

    def rows(width):
        return pl.BlockSpec((1, tm, width), lambda bi, ti: (bi, ti, 0))

    consts = (wpa, wout, nf, wup, cw, cb, wdown, nfin)
    return pl.pallas_call(
        functools.partial(_conv_ffn_body, final_norm=final_norm),
        grid=(b, s // tm),
        in_specs=[rows(d_model), rows(oa.shape[2]), rows(d_model), rows(d_model),
                  pl.BlockSpec((SUBLANES, 2 * d_ff), lambda bi, ti: ((bi * nt + ti) // (SUBLANES // 2), 0))]
                 + [_const_spec(a.shape) for a in consts],
        out_specs=rows(d_model),
        out_shape=jax.ShapeDtypeStruct(x.shape, F32),
        scratch_shapes=[pltpu.VMEM((tm, d_model), F32)] + [pltpu.VMEM((d_model // LANES, tm, LANES), F32)] * 2
        + [pltpu.VMEM((tm, d_ff), BF16)],
        compiler_params=_params(2),
        name="conv_ffn",
    )(x, oa, sa, mb, halo, *consts)


def _rel_bucket(rel):
    nb = NUM_BUCKETS // 2
    max_exact = nb // 2
    ret = np.where(rel > 0, nb, 0)
    n = np.abs(rel)
    nf = np.maximum(n, 1).astype(np.float32)
    large = max_exact + (np.log(nf / max_exact) / math.log(MAX_DISTANCE / max_exact)
                         * (nb - max_exact)).astype(np.int32)
    large = np.minimum(large, nb - 1)
    return (ret + np.where(n < max_exact, n, large)).astype(np.int32)


def _score_bias(rel_bias):
    period = Q_BLOCK + KEY_BLOCK
    rel = np.arange(period) - (Q_BLOCK - 1) - HALF_WINDOW
    in_band = jnp.asarray(np.abs(rel) <= HALF_WINDOW)
    col = np.arange(KEY_BLOCK)
    col_masks = [np.zeros(KEY_BLOCK, bool), col < HALF_WINDOW, col >= Q_BLOCK + HALF_WINDOW]
    col_masks.append(col_masks[1] | col_masks[2])
    out = []
    for g, (_, dil) in enumerate(DILATED_GROUPS):
        tab = rel_bias[:, g * HEADS_PER_GROUP:(g + 1) * HEADS_PER_GROUP].astype(F32)
        by_offset = jnp.where(in_band[:, None], tab[jnp.asarray(_rel_bucket(rel * dil))] * LOG2E, MASK_VALUE)
        by_offset = jnp.roll(by_offset.T, -(Q_BLOCK - 1), axis=1)
        bias = jnp.tile(by_offset, (1, Q_BLOCK))[:, :Q_BLOCK * (period - 1)]
        bias = bias.reshape(HEADS_PER_GROUP, Q_BLOCK, period - 1)[:, :, :KEY_BLOCK]
        bias = bias.reshape(HEADS_PER_GROUP // 2, 2 * Q_BLOCK, KEY_BLOCK)
        out.append(jnp.stack([jnp.where(jnp.asarray(cm), MASK_VALUE, bias) for cm in col_masks]))
    return out


def _layer_params(l, norm_mix, w_in, v_gain, w_s, b_s, w_proj_a, w_proj_b, w_out, norm_ffn, w_up,
                  conv_w, conv_b, w_down):
    attn_w = N_GROUPS * GROUP_WIDTH
    width = v_gain.shape[1]
    d_ff = w_down.shape[1]
    w = w_in[l].astype(BF16)
    wq = [jnp.concatenate([w[:, part * attn_w + g * GROUP_WIDTH: part * attn_w + (g + 1) * GROUP_WIDTH]
                           for part in range(3)], axis=1) for g in range(N_GROUPS)]
    wuv = w[:, 3 * attn_w: 3 * attn_w + 2 * width]
    wg = w[:, 3 * attn_w + 2 * width:]
    zb = jnp.repeat(b_s[l].T.astype(F32), width // SG_GROUPS, axis=1)

    def pair_columns(a):
        half = FF_CHUNK // 2
        return jnp.concatenate([a[..., base + c * half: base + (c + 1) * half]
                                for c in range(d_ff // half) for base in (0, d_ff)], axis=-1)

    return dict(
        nm=norm_mix[l][None, :], wq=wq, wuv=wuv, wg=wg, vgain=v_gain[l][None, :],
        ws=w_s[l].astype(BF16), zb=zb, wpb=w_proj_b[l].astype(BF16),
        wpa=w_proj_a[l].astype(BF16), wout=w_out[l].astype(BF16),
        nf=norm_ffn[l][None, :], wup=pair_columns(w_up[l].astype(BF16)), cw=pair_columns(conv_w[l]),
        cb=pair_columns(conv_b[l][None, :]), wdown=w_down[l].astype(BF16),
    )


def _trunk(x, layers, biases, nfin):
    seq_len = x.shape[1]
    for i, p in enumerate(layers):
        *qkvs, sa, mb = _mixer_in(x, p["nm"], p["wq"], p["wuv"], p["wg"], p["vgain"], p["ws"], p["zb"],
                                  p["wpb"])
        oa = _attention(qkvs, biases, seq_len)
        x = _conv_ffn(x, oa, sa, mb, p["wpa"], p["wout"], p["nf"], p["wup"], p["cw"], p["cb"], p["wdown"],
                      nfin, final_norm=(i == len(layers) - 1))
    return x


def kernel(x_prompt, x_sample, rel_bias, norm_mix, w_in, v_gain, w_s, b_s, w_proj_a, w_proj_b, w_out,
           norm_ffn, w_up, conv_w, conv_b, w_down, norm_final):
    depth = w_in.shape[0]
    layers = [_layer_params(l, norm_mix, w_in, v_gain, w_s, b_s, w_proj_a, w_proj_b, w_out, norm_ffn,
                            w_up, conv_w, conv_b, w_down) for l in range(depth)]
    biases = _score_bias(rel_bias)
    nfin = norm_final[None, :]
    return (_trunk(x_prompt, layers, biases, nfin), _trunk(x_sample, layers, biases, nfin))
```

```python
import functools
import math

import jax
import jax.numpy as jnp
import numpy as np
from jax import lax
from jax.experimental import pallas as pl
from jax.experimental.pallas import tpu as pltpu

F32 = jnp.float32
BF16 = jnp.bfloat16

HEAD_DIM = 64
HEADS_PER_GROUP = 4
GROUP_WIDTH = HEADS_PER_GROUP * HEAD_DIM
DILATED_GROUPS = ((128, 1), (512, 4), (2048, 16))
N_GROUPS = len(DILATED_GROUPS)
HALF_WINDOW = 64
NUM_BUCKETS = 32
MAX_DISTANCE = 1024
SG_CHUNK = 128
SG_GROUPS = 8
CONV_WIDTH = 3
EPS = 1e-6
MASK_VALUE = -1e30
LOG2E = math.log2(math.e)

LANES = 128
SUBLANES = 8
BF16_ROWS = 16
VMEM_LIMIT_BYTES = 56 * 1024 * 1024

ROW_TILE = 512
ATTN_TILE = 2048
Q_BLOCK = 128
KEY_BLOCK = Q_BLOCK + 2 * HALF_WINDOW
FF_CHUNK = 2 * LANES
MERGE_ROWS = 256
DEINTERLEAVE_STRIDE = 4


def _rmsnorm(x, gain):
    y = x * lax.rsqrt(jnp.mean(x * x, axis=-1, keepdims=True) + EPS)
    return y * gain


def _gelu_tanh(x):
    c = math.sqrt(2.0 / math.pi)
    return 0.5 * x * (1.0 + jnp.tanh(c * (x + 0.044715 * (x * x * x))))


def _gelu_tanh_exp(x):
    k0 = -2.0 * math.sqrt(2.0 / math.pi) * LOG2E
    return x / (1.0 + jnp.exp2(x * (k0 + (0.044715 * k0) * (x * x))))


def _sigmoid(x):
    return 0.5 * jnp.tanh(0.5 * x) + 0.5


def _const_spec(shape):
    zeros = (0,) * len(shape)
    return pl.BlockSpec(shape, lambda *_: zeros, pipeline_mode=pl.Buffered(1))


def _params(n_grid_axes):
    return pltpu.CompilerParams(
        dimension_semantics=("arbitrary",) * n_grid_axes,
        vmem_limit_bytes=VMEM_LIMIT_BYTES,
    )


def _mixer_in_body(x_ref, nm_ref, wq0_ref, wq1_ref, wq2_ref, wuv_ref, wg_ref, vgain_ref, ws_ref,
                   zb_ref, wpb_ref, qkv0_ref, qkv1_ref, qkv2_ref, sa_ref, mb_ref, perm_ref, perm2_ref):
    tm = x_ref.shape[1]
    width = vgain_ref.shape[1]
    h = _rmsnorm(x_ref[0], nm_ref[...]).astype(BF16)

    uv = jnp.dot(h, wuv_ref[...], preferred_element_type=F32)
    gates = jnp.dot(h, wg_ref[...], preferred_element_type=F32)

    for (_, dil), w_ref, out_ref in zip(DILATED_GROUPS, (wq0_ref, wq1_ref, wq2_ref),
                                        (qkv0_ref, qkv1_ref, qkv2_ref)):
        r = jnp.dot(h, w_ref[...], preferred_element_type=F32)
        r = jnp.concatenate([r[:, :GROUP_WIDTH] * (LOG2E * HEAD_DIM ** -0.5), r[:, GROUP_WIDTH:]], axis=1)
        if dil == 1:
            out_ref[0, 0] = r.astype(BF16)
            continue
        for c in range(perm_ref.shape[0]):
            perm_ref[c] = r[:, c * LANES:(c + 1) * LANES]
        src_ref, src_dil = perm_ref, dil
        if dil > DEINTERLEAVE_STRIDE:
            n = tm // DEINTERLEAVE_STRIDE
            for c in range(perm_ref.shape[0]):
                for j in range(DEINTERLEAVE_STRIDE):
                    perm2_ref[c, j * n:(j + 1) * n, :] = perm_ref[c, pl.ds(j, n, stride=DEINTERLEAVE_STRIDE), :]
            src_ref, src_dil = perm2_ref, dil // DEINTERLEAVE_STRIDE
        for res in range(dil):
            j, k = res % (dil // src_dil), res // (dil // src_dil)
            start = j * (tm // (dil // src_dil)) + k
            for c in range(perm_ref.shape[0]):
                out_ref[0, res, :, c * LANES:(c + 1) * LANES] = (
                    src_ref[c, pl.ds(start, tm // dil, stride=src_dil), :].astype(BF16))

    d_model = gates.shape[1] // 2
    sig = _sigmoid(gates.astype(BF16))
    sa_ref[0] = sig[:, :d_model]

    uv = _gelu_tanh(uv.astype(BF16))
    u = uv[:, :width]
    vn = _rmsnorm(uv[:, width:].astype(F32), vgain_ref[...]).astype(BF16)
    gch = width // SG_GROUPS
    n_chunks = tm // SG_CHUNK
    mixed = []
    for g in range(SG_GROUPS):
        blocks = jnp.concatenate([vn[c * SG_CHUNK:(c + 1) * SG_CHUNK, g * gch:(g + 1) * gch]
                                  for c in range(n_chunks)], axis=1)
        mixed.append(jnp.dot(ws_ref[g], blocks, preferred_element_type=F32))
    z = jnp.concatenate(
        [jnp.concatenate([mixed[g][:, c * gch:(c + 1) * gch] for g in range(SG_GROUPS)], axis=1) + zb_ref[...]
         for c in range(n_chunks)], axis=0)
    y_b = jnp.dot(u * z.astype(BF16), wpb_ref[...], preferred_element_type=F32)
    mb_ref[0] = sig[:, d_model:] * y_b.astype(BF16)


def _mixer_in(x, nm, wq, wuv, wg, vgain, ws, zb, wpb):
    b, s, d_model = x.shape
    tm = min(ROW_TILE, s)
    assert s % tm == 0 and tm % SG_CHUNK == 0
    max_dil = max(d for _, d in DILATED_GROUPS)
    assert tm % (max_dil * BF16_ROWS) == 0
    qkv_w = 3 * GROUP_WIDTH
    out_shape = [jax.ShapeDtypeStruct((b, dil, s // dil, qkv_w), BF16) for _, dil in DILATED_GROUPS]
    out_shape += [jax.ShapeDtypeStruct((b, s, d_model), BF16)] * 2
    out_specs = [pl.BlockSpec((1, dil, tm // dil, qkv_w), lambda bi, ti: (bi, 0, ti, 0))
                 for _, dil in DILATED_GROUPS]
    out_specs += [pl.BlockSpec((1, tm, d_model), lambda bi, ti: (bi, ti, 0))] * 2
    in_specs = [pl.BlockSpec((1, tm, d_model), lambda bi, ti: (bi, ti, 0)), _const_spec(nm.shape)]
    in_specs += [_const_spec(w.shape) for w in wq]
    in_specs += [_const_spec(a.shape) for a in (wuv, wg, vgain, ws, zb, wpb)]
    return pl.pallas_call(
        _mixer_in_body,
        grid=(b, s // tm),
        in_specs=in_specs,
        out_specs=out_specs,
        out_shape=out_shape,
        scratch_shapes=[pltpu.VMEM((qkv_w // LANES, tm, LANES), F32)] * 2,
        compiler_params=_params(2),
        name="mixer_in",
    )(x, nm, *wq, wuv, wg, vgain, ws, zb, wpb)


def _attend_pair(q_pair, k_pair, v_pair, bias, low_half):
    zero = jnp.zeros_like(q_pair)
    q_stack = jnp.concatenate([jnp.where(low_half, q_pair, zero), jnp.where(low_half, zero, q_pair)],
                              axis=0)
    s = lax.dot_general(q_stack, k_pair, (((1,), (1,)), ((), ())), preferred_element_type=F32)
    s = s + bias
    m = jnp.max(s, axis=-1, keepdims=True)
    p = jnp.exp2(s - m)
    denom = jnp.sum(p, axis=-1, keepdims=True)
    pv = jnp.dot(p.astype(BF16), v_pair, preferred_element_type=F32)
    return (jnp.where(low_half, pv[:Q_BLOCK], pv[Q_BLOCK:]), jnp.where(low_half, m[:Q_BLOCK], m[Q_BLOCK:]),
            jnp.where(low_half, denom[:Q_BLOCK], denom[Q_BLOCK:]))


def _attention_body(*refs):
    n_in = 8 * N_GROUPS
    in_refs, out_ref, scratch = refs[:n_in], refs[n_in], refs[n_in + 1:]
    ti = pl.program_id(1)
    n_tiles = pl.num_programs(1)
    tq = out_ref.shape[1]
    lane = lax.broadcasted_iota(jnp.int32, (1, LANES), 1)
    low_half = lane < HEAD_DIM

    for g, (_, dil) in enumerate(DILATED_GROUPS):
        (q_ref, k_ref, kp_ref, kn_ref, v_ref, vp_ref, vn_ref, bias_ref) = in_refs[8 * g:8 * g + 8]
        o_ref, m_ref, den_ref = scratch[3 * g:3 * g + 3]
        tl = tq // dil
        n_blocks = tl // Q_BLOCK

        def window(main_ref, before_ref, after_ref, res, jb, tl=tl):
            lo, hi = jb * Q_BLOCK - HALF_WINDOW, (jb + 1) * Q_BLOCK + HALF_WINDOW
            parts = [main_ref[0, res, max(lo, 0):min(hi, tl), :]]
            if lo < 0:
                parts.insert(0, before_ref[0, res])
            if hi > tl:
                parts.append(after_ref[0, res])
            return jnp.concatenate(parts, axis=0)

        for unit in range(dil * n_blocks):
            res, jb = divmod(unit, n_blocks)
            row0 = jb * Q_BLOCK
            q_blk = q_ref[0, res, row0:row0 + Q_BLOCK, :]
            k_win = window(k_ref, kp_ref, kn_ref, res, jb)
            v_win = window(v_ref, vp_ref, vn_ref, res, jb)
            variant = 0
            if jb == 0:
                variant = variant + (ti == 0).astype(jnp.int32)
            if jb == n_blocks - 1:
                variant = variant + 2 * (ti == n_tiles - 1).astype(jnp.int32)
            parts = [_attend_pair(q_blk[:, hp * LANES:(hp + 1) * LANES], k_win[:, hp * LANES:(hp + 1) * LANES],
                                  v_win[:, hp * LANES:(hp + 1) * LANES], bias_ref[variant, hp], low_half)
                     for hp in range(HEADS_PER_GROUP // 2)]
            rows = pl.ds(row0 * dil + res, Q_BLOCK, stride=dil) if dil > 1 else slice(row0, row0 + Q_BLOCK)
            for c, (o_pair, m_pair, den_pair) in enumerate(parts):
                o_ref[c, rows, :] = o_pair
                m_ref[c, rows, :] = m_pair
                den_ref[c, rows, :] = den_pair

    def merge(it, carry):
        rows = pl.ds(pl.multiple_of(it * MERGE_ROWS, MERGE_ROWS), MERGE_ROWS)
        for c in range(GROUP_WIDTH // LANES):
            ms = [scratch[3 * g + 1][c, rows, :] for g in range(N_GROUPS)]
            mx = functools.reduce(jnp.maximum, ms)
            ws = [jnp.exp2(m - mx) for m in ms]
            num = sum(w * scratch[3 * g][c, rows, :] for g, w in enumerate(ws))
            den = sum(w * scratch[3 * g + 2][c, rows, :] for g, w in enumerate(ws))
            out_ref[0, rows, c * LANES:(c + 1) * LANES] = (num / den).astype(BF16)
        return carry

    lax.fori_loop(0, tq // MERGE_ROWS, merge, 0)


def _attention(qkvs, biases, seq_len):
    b = qkvs[0].shape[0]
    tq = min(ATTN_TILE, seq_len)
    assert seq_len % tq == 0
    in_specs, args, scratch = [], [], []
    for (_, dil), qkv, bias in zip(DILATED_GROUPS, qkvs, biases):
        tl = tq // dil
        assert tl % Q_BLOCK == 0
        halo_per_tile = tl // HALF_WINDOW
        n_halo = seq_len // dil // HALF_WINDOW

        def main(col):
            return pl.BlockSpec((1, dil, tl, GROUP_WIDTH), lambda bi, ti, col=col: (bi, 0, ti, col))

        def before(col, hpt=halo_per_tile):
            return pl.BlockSpec((1, dil, HALF_WINDOW, GROUP_WIDTH),
                                lambda bi, ti, col=col: (bi, 0, jnp.maximum(ti * hpt - 1, 0), col))

        def after(col, hpt=halo_per_tile, n_halo=n_halo):
            return pl.BlockSpec((1, dil, HALF_WINDOW, GROUP_WIDTH),
                                lambda bi, ti, col=col: (bi, 0, jnp.minimum((ti + 1) * hpt, n_halo - 1), col))

        in_specs += [main(0), main(1), before(1), after(1), main(2), before(2), after(2),
                     _const_spec(bias.shape)]
        args += [qkv] * 7 + [bias]
        acc = (GROUP_WIDTH // LANES, tq, LANES)
        scratch += [pltpu.VMEM(acc, F32)] * 3
    return pl.pallas_call(
        _attention_body,
        grid=(b, seq_len // tq),
        in_specs=in_specs,
        out_specs=pl.BlockSpec((1, tq, GROUP_WIDTH), lambda bi, ti: (bi, ti, 0)),
        out_shape=jax.ShapeDtypeStruct((b, seq_len, GROUP_WIDTH), BF16),
        scratch_shapes=scratch,
        compiler_params=_params(2),
        name="attention",
    )(*args)


def _mixer_out(x, oa, sa, mb, wpa_ref, wout_ref):
    y_a = jnp.dot(oa, wpa_ref[...], preferred_element_type=F32)
    merged = sa * y_a.astype(BF16) + mb
    return x + jnp.dot(merged, wout_ref[...], preferred_element_type=F32)


def _ffn_halo_body(x_ref, oa_ref, sa_ref, mb_ref, wpa_ref, wout_ref, nf_ref, wup_ref, out_ref):
    x1 = _mixer_out(x_ref[...], oa_ref[...], sa_ref[...], mb_ref[...], wpa_ref, wout_ref)
    h = _rmsnorm(x1, nf_ref[...]).astype(BF16)
    out_ref[...] = jnp.dot(h, wup_ref[...], preferred_element_type=F32)


def _ffn_halo(rows_of, wpa, wout, nf, wup, tm):
    b, s, _ = rows_of[0].shape
    nt = s // tm

    def edge_rows(a):
        at = a.reshape(b, nt, tm, a.shape[2])
        zero = jnp.zeros((b, 1, a.shape[2]), a.dtype)
        before = jnp.concatenate([zero, at[:, :-1, tm - 1, :]], axis=1)
        after = jnp.concatenate([at[:, 1:, 0, :], zero], axis=1)
        return jnp.stack([before, after], axis=2).reshape(b * nt * 2, a.shape[2])

    return pl.pallas_call(
        _ffn_halo_body,
        out_shape=jax.ShapeDtypeStruct((b * nt * 2, wup.shape[1]), F32),
        compiler_params=pltpu.CompilerParams(vmem_limit_bytes=VMEM_LIMIT_BYTES),
        name="ffn_halo",
    )(*[edge_rows(a) for a in rows_of], wpa, wout, nf, wup)


def _conv_ffn_body(x_ref, oa_ref, sa_ref, mb_ref, halo_ref, wpa_ref, wout_ref, nf_ref, wup_ref, cw_ref,
                   cb_ref, wdown_ref, nfin_ref, out_ref, x1_ref, perm_ref, perm2_ref, act_ref, *, final_norm):
    tm = x_ref.shape[1]
    n_slabs = perm_ref.shape[0]
    quarter = tm // DEINTERLEAVE_STRIDE
    tile = pl.program_id(0) * pl.num_programs(1) + pl.program_id(1)
    halo_row = (tile % (SUBLANES // 2)) * 2
    x1_ref[...] = _mixer_out(x_ref[0], oa_ref[0], sa_ref[0], mb_ref[0], wpa_ref, wout_ref)
    for q in range(DEINTERLEAVE_STRIDE):
        for c in range(n_slabs):
            perm2_ref[c, pl.ds(q, quarter, stride=DEINTERLEAVE_STRIDE), :] = (
                x1_ref[q * quarter:(q + 1) * quarter, c * LANES:(c + 1) * LANES])
    for q in range(DEINTERLEAVE_STRIDE):
        for c in range(n_slabs):
            perm_ref[c, pl.ds(q, quarter, stride=DEINTERLEAVE_STRIDE), :] = perm2_ref[c, q * quarter:(q + 1) * quarter, :]
    xp = jnp.concatenate([perm_ref[c] for c in range(n_slabs)], axis=1)
    h = _rmsnorm(xp, nf_ref[...]).astype(BF16)
    row = lax.broadcasted_iota(jnp.int32, (BF16_ROWS, 1), 0)
    half = FF_CHUNK // 2
    for c in range(wup_ref.shape[1] // FF_CHUNK):
        cols = slice(c * FF_CHUNK, (c + 1) * FF_CHUNK)
        a = jnp.dot(h, wup_ref[:, cols], preferred_element_type=F32)
        a16 = a.astype(BF16)
        wrap_prev = jnp.where(row == 0, halo_ref[pl.ds(halo_row, 1), cols],
                              pltpu.roll(a[-BF16_ROWS:], 1, 0)).astype(BF16)
        wrap_next = jnp.where(row == BF16_ROWS - 1, halo_ref[pl.ds(halo_row + 1, 1), cols],
                              pltpu.roll(a[:BF16_ROWS], BF16_ROWS - 1, 0)).astype(BF16)
        prev = jnp.concatenate([wrap_prev, a16[:-BF16_ROWS]], axis=0)
        nxt = jnp.concatenate([a16[BF16_ROWS:], wrap_next], axis=0)
        cw = cw_ref[:, cols].astype(BF16)
        conv = prev * cw[0:1] + a16 * cw[1:2] + nxt * cw[2:3] + cb_ref[:, cols].astype(BF16)
        act_ref[:, c * half:(c + 1) * half] = _gelu_tanh_exp(conv[:, :half]) * conv[:, half:]
    y = jnp.dot(act_ref[...], wdown_ref[...], preferred_element_type=F32)
    for c in range(n_slabs):
        perm_ref[c] = y[:, c * LANES:(c + 1) * LANES]
    for q in range(DEINTERLEAVE_STRIDE):
        for c in range(n_slabs):
            perm2_ref[c, q * quarter:(q + 1) * quarter, :] = perm_ref[c, pl.ds(q, quarter, stride=DEINTERLEAVE_STRIDE), :]
    for q in range(DEINTERLEAVE_STRIDE):
        rows = slice(q * quarter, (q + 1) * quarter)
        out = x1_ref[rows, :] + jnp.concatenate(
            [perm2_ref[c, pl.ds(q, quarter, stride=DEINTERLEAVE_STRIDE), :] for c in range(n_slabs)], axis=1)
        if final_norm:
            out = _rmsnorm(out, nfin_ref[...])
        out_ref[0, rows, :] = out


def _conv_ffn(x, oa, sa, mb, wpa, wout, nf, wup, cw, cb, wdown, nfin, final_norm):
    b, s, d_model = x.shape
    tm = min(ROW_TILE, s)
    d_ff = wdown.shape[0]
    assert s % tm == 0 and (2 * d_ff) % FF_CHUNK == 0 and tm % (BF16_ROWS * BF16_ROWS) == 0
    nt = s // tm
    assert (b * nt * 2) % SUBLANES == 0
    halo = _ffn_halo((x, oa, sa, mb), wpa, wout, nf, wup, tm)

    def rows(width):
        return pl.BlockSpec((1, tm, width), lambda bi, ti: (bi, ti, 0))

    consts = (wpa, wout, nf, wup, cw, cb, wdown, nfin)
    return pl.pallas_call(
        functools.partial(_conv_ffn_body, final_norm=final_norm),
        grid=(b, s // tm),
        in_specs=[rows(d_model), rows(oa.shape[2]), rows(d_model), rows(d_model),
                  pl.BlockSpec((SUBLANES, 2 * d_ff), lambda bi, ti: ((bi * nt + ti) // (SUBLANES // 2), 0))]
                 + [_const_spec(a.shape) for a in consts],
        out_specs=rows(d_model),
        out_shape=jax.ShapeDtypeStruct(x.shape, F32),
        scratch_shapes=[pltpu.VMEM((tm, d_model), F32)] + [pltpu.VMEM((d_model // LANES, tm, LANES), F32)] * 2
        + [pltpu.VMEM((tm, d_ff), BF16)],
        compiler_params=_params(2),
        name="conv_ffn",
    )(x, oa, sa, mb, halo, *consts)


def _rel_bucket(rel):
    nb = NUM_BUCKETS // 2
    max_exact = nb // 2
    ret = np.where(rel > 0, nb, 0)
    n = np.abs(rel)
    nf = np.maximum(n, 1).astype(np.float32)
    large = max_exact + (np.log(nf / max_exact) / math.log(MAX_DISTANCE / max_exact)
                         * (nb - max_exact)).astype(np.int32)
    large = np.minimum(large, nb - 1)
    return (ret + np.where(n < max_exact, n, large)).astype(np.int32)


def _score_bias(rel_bias):
    period = Q_BLOCK + KEY_BLOCK
    rel = np.arange(period) - (Q_BLOCK - 1) - HALF_WINDOW
    in_band = jnp.asarray(np.abs(rel) <= HALF_WINDOW)
    col = np.arange(KEY_BLOCK)
    col_masks = [np.zeros(KEY_BLOCK, bool), col < HALF_WINDOW, col >= Q_BLOCK + HALF_WINDOW]
    col_masks.append(col_masks[1] | col_masks[2])
    out = []
    for g, (_, dil) in enumerate(DILATED_GROUPS):
        tab = rel_bias[:, g * HEADS_PER_GROUP:(g + 1) * HEADS_PER_GROUP].astype(F32)
        by_offset = jnp.where(in_band[:, None], tab[jnp.asarray(_rel_bucket(rel * dil))] * LOG2E, MASK_VALUE)
        by_offset = jnp.roll(by_offset.T, -(Q_BLOCK - 1), axis=1)
        bias = jnp.tile(by_offset, (1, Q_BLOCK))[:, :Q_BLOCK * (period - 1)]
        bias = bias.reshape(HEADS_PER_GROUP, Q_BLOCK, period - 1)[:, :, :KEY_BLOCK]
        bias = bias.reshape(HEADS_PER_GROUP // 2, 2 * Q_BLOCK, KEY_BLOCK)
        out.append(jnp.stack([jnp.where(jnp.asarray(cm), MASK_VALUE, bias) for cm in col_masks]))
    return out


def _layer_params(l, norm_mix, w_in, v_gain, w_s, b_s, w_proj_a, w_proj_b, w_out, norm_ffn, w_up,
                  conv_w, conv_b, w_down):
    attn_w = N_GROUPS * GROUP_WIDTH
    width = v_gain.shape[1]
    d_ff = w_down.shape[1]
    w = w_in[l].astype(BF16)
    wq = [jnp.concatenate([w[:, part * attn_w + g * GROUP_WIDTH: part * attn_w + (g + 1) * GROUP_WIDTH]
                           for part in range(3)], axis=1) for g in range(N_GROUPS)]
    wuv = w[:, 3 * attn_w: 3 * attn_w + 2 * width]
    wg = w[:, 3 * attn_w + 2 * width:]
    zb = jnp.repeat(b_s[l].T.astype(F32), width // SG_GROUPS, axis=1)

    def pair_columns(a):
        half = FF_CHUNK // 2
        return jnp.concatenate([a[..., base + c * half: base + (c + 1) * half]
                                for c in range(d_ff // half) for base in (0, d_ff)], axis=-1)

    return dict(
        nm=norm_mix[l][None, :], wq=wq, wuv=wuv, wg=wg, vgain=v_gain[l][None, :],
        ws=w_s[l].astype(BF16), zb=zb, wpb=w_proj_b[l].astype(BF16),
        wpa=w_proj_a[l].astype(BF16), wout=w_out[l].astype(BF16),
        nf=norm_ffn[l][None, :], wup=pair_columns(w_up[l].astype(BF16)), cw=pair_columns(conv_w[l]),
        cb=pair_columns(conv_b[l][None, :]), wdown=w_down[l].astype(BF16),
    )


def _trunk(x, layers, biases, nfin):
    seq_len = x.shape[1]
    for i, p in enumerate(layers):
        *qkvs, sa, mb = _mixer_in(x, p["nm"], p["wq"], p["wuv"], p["wg"], p["vgain"], p["ws"], p["zb"],
                                  p["wpb"])
        oa = _attention(qkvs, biases, seq_len)
        x = _conv_ffn(x, oa, sa, mb, p["wpa"], p["wout"], p["nf"], p["wup"], p["cw"], p["cb"], p["wdown"],
                      nfin, final_norm=(i == len(layers) - 1))
    return x


def kernel(x_prompt, x_sample, rel_bias, norm_mix, w_in, v_gain, w_s, b_s, w_proj_a, w_proj_b, w_out,
           norm_ffn, w_up, conv_w, conv_b, w_down, norm_final):
    depth = w_in.shape[0]
    layers = [_layer_params(l, norm_mix, w_in, v_gain, w_s, b_s, w_proj_a, w_proj_b, w_out, norm_ffn,
                            w_up, conv_w, conv_b, w_down) for l in range(depth)]
    biases = _score_bias(rel_bias)
    nfin = norm_final[None, :]
    return (_trunk(x_prompt, layers, biases, nfin), _trunk(x_sample, layers, biases, nfin))
```
